```python
import math, functools
import jax, jax.numpy as jnp
from jax import lax
import numpy as np

D_MODEL = 1024
BATCH = 8
SEQ = 4096
DEPTH = 2
DEC_BATCH = 8
DEC_SEQ = 64
PAST_LEN = 1024

CHUNK = 64
Q_BLOCK = 128
FOX_HEADS = 8
FOX_HEAD_DIM = 64
FOX_DIM = FOX_HEADS * FOX_HEAD_DIM
CONV_DIM = 512
CONV_WIDTH = 3
GMLP_CHUNK = 128
GMLP_GROUPS = 8
GMLP_DIM = D_MODEL
GMLP_GROUP_DIM = GMLP_DIM // GMLP_GROUPS
MEM_LEN = 256
MEM_HEADS = 4
MEM_HEAD_DIM = 128
MEM_DIM = MEM_HEADS * MEM_HEAD_DIM
D_FF = ((8 * D_MODEL + 3 * 256 - 1) // (3 * 256)) * 256
N_EVEN = (DEPTH + 1) // 2
N_ODD = DEPTH // 2
DEEPNORM_ALPHA = (2 * DEPTH) ** 0.25
DEEPNORM_BETA = (8 * DEPTH) ** -0.25
LN_EPS = 1e-5
EVEN_SPLITS = (FOX_DIM, 2 * FOX_DIM, 3 * FOX_DIM, 3 * FOX_DIM + FOX_HEADS,
               3 * FOX_DIM + FOX_HEADS + CONV_DIM, 3 * FOX_DIM + FOX_HEADS + 2 * CONV_DIM)
EVEN_IN_DIM = 3 * FOX_DIM + FOX_HEADS + 3 * CONV_DIM

kernel_name = 'fox_shortconv_gmlp_stream_step'


def layer_norm(x, g, b, out_dtype):
    xf = x.astype(jnp.float32)
    mu = jnp.mean(xf, axis=-1, keepdims=True)
    var = jnp.mean(jnp.square(xf - mu), axis=-1, keepdims=True)
    return ((xf - mu) * lax.rsqrt(var + LN_EPS) * g.astype(jnp.float32) + b.astype(jnp.float32)).astype(out_dtype)


def post_norm(x, y, g, b):
    return layer_norm(DEEPNORM_ALPHA * x.astype(jnp.float32) + y.astype(jnp.float32), g, b, x.dtype)


def even_projections(x, w_in, b_f):
    bn, t = x.shape[:2]
    z = x @ w_in
    q, k, v, f, h, bg, cg = jnp.split(z, EVEN_SPLITS, axis=-1)
    hs = (bn, t, FOX_HEADS, FOX_HEAD_DIM)
    logf = jax.nn.log_sigmoid(f.astype(jnp.float32) + b_f.astype(jnp.float32))
    return q.reshape(hs), k.reshape(hs), v.reshape(hs), logf, cg * h, bg


def fox_attention_prompt(q, k, v, logf):
    bn, s_len = q.shape[:2]
    scale = FOX_HEAD_DIM ** -0.5
    c_t = jnp.swapaxes(jnp.cumsum(logf, axis=1), 1, 2)
    kpos = jnp.arange(s_len)

    def block(i):
        start = i * Q_BLOCK
        qb = lax.dynamic_slice_in_dim(q, start, Q_BLOCK, axis=1)
        cq = lax.dynamic_slice_in_dim(c_t, start, Q_BLOCK, axis=2)
        s = jnp.einsum('bqhd,bkhd->bhqk', qb, k, preferred_element_type=jnp.float32) * scale
        s = s + (cq[..., :, None] - c_t[..., None, :])
        qpos = start + jnp.arange(Q_BLOCK)
        s = jnp.where(kpos[None, :] <= qpos[:, None], s, -jnp.inf)
        p = jax.nn.softmax(s, axis=-1)
        return jnp.einsum('bhqk,bkhd->bqhd', p.astype(v.dtype), v)

    out = lax.map(block, jnp.arange(s_len // Q_BLOCK))
    return jnp.swapaxes(out, 0, 1).reshape(bn, s_len, FOX_DIM)


def fox_attention_sample(q, k_new, v_new, logf_new, k_cache, v_cache, logf_cache):
    bn, t = q.shape[:2]
    p_len = k_cache.shape[1]
    scale = FOX_HEAD_DIM ** -0.5
    k = jnp.concatenate([k_cache.astype(k_new.dtype), k_new], axis=1)
    v = jnp.concatenate([v_cache.astype(v_new.dtype), v_new], axis=1)
    logf = jnp.concatenate([logf_cache.astype(jnp.float32), logf_new], axis=1)
    c_t = jnp.swapaxes(jnp.cumsum(logf, axis=1), 1, 2)
    cq = c_t[..., p_len:]
    s = jnp.einsum('bqhd,bkhd->bhqk', q, k, preferred_element_type=jnp.float32) * scale
    s = s + (cq[..., :, None] - c_t[..., None, :])
    kpos = jnp.arange(p_len + t)
    qpos = p_len + jnp.arange(t)
    s = jnp.where(kpos[None, :] <= qpos[:, None], s, -jnp.inf)
    p = jax.nn.softmax(s, axis=-1)
    return jnp.einsum('bhqk,bkhd->bqhd', p.astype(v.dtype), v).reshape(bn, t, FOX_DIM)


def causal_short_conv(x_hist, w):
    return lax.conv_general_dilated(x_hist, w[:, None, :].astype(x_hist.dtype), (1,), 'VALID',
                                    dimension_numbers=('NWC', 'WIO', 'NWC'),
                                    feature_group_count=x_hist.shape[-1])


def gmlp_gate_inputs(x, w_in, b_in, g, b):
    z = jax.nn.gelu(x @ w_in + b_in, approximate=False)
    u, v = jnp.split(z, 2, axis=-1)
    return u, layer_norm(v, g, b, v.dtype)


def gmlp_spatial_prompt(v, w_s, b_s):
    bn, s_len = v.shape[:2]
    vb = v.reshape(bn, s_len // GMLP_CHUNK, GMLP_CHUNK, GMLP_GROUPS, GMLP_GROUP_DIM)
    w = w_s * jnp.tril(jnp.ones((GMLP_CHUNK, GMLP_CHUNK), w_s.dtype))
    mixed = jnp.einsum('gts,bnsgc->bntgc', w, vb) + b_s.T[None, None, :, :, None]
    return mixed.reshape(bn, s_len, GMLP_DIM)


def gmlp_spatial_sample(v, w_s, b_s):
    bn, t = v.shape[:2]
    w = (w_s * jnp.tril(jnp.ones((GMLP_CHUNK, GMLP_CHUNK), w_s.dtype)))[:, :t, :t]
    vb = v.reshape(bn, t, GMLP_GROUPS, GMLP_GROUP_DIM)
    mixed = jnp.einsum('gts,bsgc->btgc', w, vb) + b_s[:, :t].T[None, :, :, None]
    return mixed.reshape(bn, t, GMLP_DIM)


def memory_kv(mem, w_k, w_v):
    bn, m = mem.shape[:2]
    hs = (bn, m, MEM_HEADS, MEM_HEAD_DIM)
    return (mem @ w_k).reshape(hs), (mem @ w_v).reshape(hs)


def memory_attention(x, mk, mv, w_q, w_o):
    bn, t = x.shape[:2]
    q = (x @ w_q).reshape(bn, t, MEM_HEADS, MEM_HEAD_DIM)
    s = jnp.einsum('bthd,bmhd->bhtm', q, mk.astype(q.dtype), preferred_element_type=jnp.float32) * (MEM_HEAD_DIM ** -0.5)
    p = jax.nn.softmax(s, axis=-1)
    o = jnp.einsum('bhtm,bmhd->bthd', p.astype(q.dtype), mv.astype(q.dtype))
    return o.reshape(bn, t, MEM_DIM) @ w_o


def swiglu(x, w_gate, w_up, w_down):
    return (jax.nn.silu(x @ w_gate) * (x @ w_up)) @ w_down


def setup_inputs(seed: int = 0) -> dict:
    key = jax.random.key(seed)
    ks = iter(jax.random.split(key, 40))
    nrm = lambda shape, s=1.0: jax.random.normal(next(ks), shape, jnp.float32) * s
    d = D_MODEL
    return {
        'x_prompt': nrm((BATCH, SEQ, d)),
        'x_sample': nrm((DEC_BATCH, DEC_SEQ, d)),
        'cache_fox_k': nrm((N_EVEN, DEC_BATCH, PAST_LEN, FOX_HEADS, FOX_HEAD_DIM)),
        'cache_fox_v': nrm((N_EVEN, DEC_BATCH, PAST_LEN, FOX_HEADS, FOX_HEAD_DIM)),
        'cache_fox_logf': jax.nn.log_sigmoid(nrm((N_EVEN, DEC_BATCH, PAST_LEN, FOX_HEADS)) + 2.5),
        'state_conv': nrm((N_EVEN, DEC_BATCH, CONV_WIDTH - 1, CONV_DIM), 0.5),
        'cache_mem_k': nrm((DEPTH, DEC_BATCH, MEM_LEN, MEM_HEADS, MEM_HEAD_DIM)),
        'cache_mem_v': nrm((DEPTH, DEC_BATCH, MEM_LEN, MEM_HEADS, MEM_HEAD_DIM)),
        'mem_prompt': nrm((BATCH, MEM_LEN, d)),
        'w_in_even': nrm((N_EVEN, d, EVEN_IN_DIM), d ** -0.5),
        'b_forget': jax.random.uniform(next(ks), (N_EVEN, FOX_HEADS), jnp.float32, 1.0, 4.0),
        'conv_w': nrm((N_EVEN, CONV_WIDTH, CONV_DIM), CONV_WIDTH ** -0.5),
        'w_out_even': nrm((N_EVEN, FOX_DIM + CONV_DIM, d), (FOX_DIM + CONV_DIM) ** -0.5 * DEEPNORM_BETA),
        'w_in_odd': nrm((N_ODD, d, 2 * GMLP_DIM), d ** -0.5),
        'b_in_odd': nrm((N_ODD, 2 * GMLP_DIM), 0.02),
        'gmlp_norm_g': 1.0 + nrm((N_ODD, GMLP_DIM), 0.02),
        'gmlp_norm_b': nrm((N_ODD, GMLP_DIM), 0.02),
        'gmlp_w_s': nrm((N_ODD, GMLP_GROUPS, GMLP_CHUNK, GMLP_CHUNK), GMLP_CHUNK ** -0.5),
        'gmlp_b_s': 1.0 + nrm((N_ODD, GMLP_GROUPS, GMLP_CHUNK), 0.02),
        'w_out_odd': nrm((N_ODD, GMLP_DIM, d), GMLP_DIM ** -0.5 * DEEPNORM_BETA),
        'mem_w_q': nrm((DEPTH, d, MEM_DIM), d ** -0.5),
        'mem_w_k': nrm((DEPTH, d, MEM_DIM), d ** -0.5),
        'mem_w_v': nrm((DEPTH, d, MEM_DIM), d ** -0.5),
        'mem_w_o': nrm((DEPTH, MEM_DIM, d), MEM_DIM ** -0.5 * DEEPNORM_BETA),
        'ffn_w_gate': nrm((DEPTH, d, D_FF), d ** -0.5),
        'ffn_w_up': nrm((DEPTH, d, D_FF), d ** -0.5),
        'ffn_w_down': nrm((DEPTH, D_FF, d), D_FF ** -0.5 * DEEPNORM_BETA),
        'ln_g': 1.0 + nrm((DEPTH, 3, d), 0.02),
        'ln_b': nrm((DEPTH, 3, d), 0.02),
    }


def reference(x_prompt, x_sample, cache_fox_k, cache_fox_v, cache_fox_logf, state_conv,
              cache_mem_k, cache_mem_v, mem_prompt,
              w_in_even, b_forget, conv_w, w_out_even,
              w_in_odd, b_in_odd, gmlp_norm_g, gmlp_norm_b, gmlp_w_s, gmlp_b_s, w_out_odd,
              mem_w_q, mem_w_k, mem_w_v, mem_w_o,
              ffn_w_gate, ffn_w_up, ffn_w_down, ln_g, ln_b):
    fk_p, fv_p, fl_p, cs_p, mk_p, mv_p = [], [], [], [], [], []
    fk_s, fv_s, fl_s, cs_s, gv_s = [], [], [], [], []
    yp, ys = x_prompt, x_sample
    for layer in range(DEPTH):
        if layer % 2 == 0:
            e = layer // 2
            q, k, v, logf, pre, bg = even_projections(yp, w_in_even[e], b_forget[e])
            att = fox_attention_prompt(q, k, v, logf)
            hist = jnp.pad(pre, ((0, 0), (CONV_WIDTH - 1, 0), (0, 0)))
            conv = bg * causal_short_conv(hist, conv_w[e])
            mix_p = jnp.concatenate([att, conv], axis=-1) @ w_out_even[e]
            fk_p.append(k)
            fv_p.append(v)
            fl_p.append(logf)
            cs_p.append(hist[:, -(CONV_WIDTH - 1):])
            q, k, v, logf, pre, bg = even_projections(ys, w_in_even[e], b_forget[e])
            att = fox_attention_sample(q, k, v, logf, cache_fox_k[e], cache_fox_v[e], cache_fox_logf[e])
            hist = jnp.concatenate([state_conv[e].astype(pre.dtype), pre], axis=1)
            conv = bg * causal_short_conv(hist, conv_w[e])
            mix_s = jnp.concatenate([att, conv], axis=-1) @ w_out_even[e]
            fk_s.append(k)
            fv_s.append(v)
            fl_s.append(logf)
            cs_s.append(hist[:, -(CONV_WIDTH - 1):])
        else:
            o = layer // 2
            u, vn = gmlp_gate_inputs(yp, w_in_odd[o], b_in_odd[o], gmlp_norm_g[o], gmlp_norm_b[o])
            mix_p = (u * gmlp_spatial_prompt(vn, gmlp_w_s[o], gmlp_b_s[o])) @ w_out_odd[o]
            u, vn = gmlp_gate_inputs(ys, w_in_odd[o], b_in_odd[o], gmlp_norm_g[o], gmlp_norm_b[o])
            mix_s = (u * gmlp_spatial_sample(vn, gmlp_w_s[o], gmlp_b_s[o])) @ w_out_odd[o]
            gv_s.append(vn)
        yp = post_norm(yp, mix_p, ln_g[layer, 0], ln_b[layer, 0])
        ys = post_norm(ys, mix_s, ln_g[layer, 0], ln_b[layer, 0])
        mk, mv = memory_kv(mem_prompt, mem_w_k[layer], mem_w_v[layer])
        mk_p.append(mk)
        mv_p.append(mv)
        yp = post_norm(yp, memory_attention(yp, mk, mv, mem_w_q[layer], mem_w_o[layer]), ln_g[layer, 1], ln_b[layer, 1])
        ys = post_norm(ys, memory_attention(ys, cache_mem_k[layer], cache_mem_v[layer], mem_w_q[layer], mem_w_o[layer]), ln_g[layer, 1], ln_b[layer, 1])
        yp = post_norm(yp, swiglu(yp, ffn_w_gate[layer], ffn_w_up[layer], ffn_w_down[layer]), ln_g[layer, 2], ln_b[layer, 2])
        ys = post_norm(ys, swiglu(ys, ffn_w_gate[layer], ffn_w_up[layer], ffn_w_down[layer]), ln_g[layer, 2], ln_b[layer, 2])
    fox_k_prompt = jnp.stack(fk_p)
    fox_v_prompt = jnp.stack(fv_p)
    fox_logf_prompt = jnp.stack(fl_p)
    conv_state_prompt = jnp.stack(cs_p)
    mem_k_prompt = jnp.stack(mk_p)
    mem_v_prompt = jnp.stack(mv_p)
    fox_k_sample = jnp.stack(fk_s)
    fox_v_sample = jnp.stack(fv_s)
    fox_logf_sample = jnp.stack(fl_s)
    conv_state_sample = jnp.stack(cs_s)
    gmlp_v_sample = jnp.stack(gv_s)
    return (yp, ys, fox_k_prompt, fox_v_prompt, fox_logf_prompt, conv_state_prompt, mem_k_prompt, mem_v_prompt,
            fox_k_sample, fox_v_sample, fox_logf_sample, conv_state_sample, gmlp_v_sample)
```

```python
import functools

import jax
import jax.numpy as jnp
from jax import lax
from jax.experimental import pallas as pl
from jax.experimental.pallas import tpu as pltpu

F32 = jnp.float32
BF16 = jnp.bfloat16

D_MODEL = 1024
FOX_HEADS = 8
FOX_HEAD_DIM = 64
FOX_DIM = FOX_HEADS * FOX_HEAD_DIM
CONV_DIM = 512
CONV_WIDTH = 3
GMLP_CHUNK = 128
GMLP_GROUPS = 8
GMLP_DIM = D_MODEL
MEM_HEADS = 4
MEM_HEAD_DIM = 128
MEM_DIM = MEM_HEADS * MEM_HEAD_DIM
LN_EPS = 1e-5

LANES = 128
SUBLANES = 8
VMEM_LIMIT_BYTES = 56 * 1024 * 1024
TOKEN_TILE = 512
FOX_TILE = 512
FFN_CHUNK = 256


def _params(*semantics):
    return pltpu.CompilerParams(dimension_semantics=semantics,
                                vmem_limit_bytes=VMEM_LIMIT_BYTES)


def _resident(shape):
    nd = len(shape)
    return pl.BlockSpec(shape, lambda *_: (0,) * nd, pipeline_mode=pl.Buffered(1))


def _dot(a, b):
    return jnp.dot(a, b, preferred_element_type=F32)


def _dot_nt(a, b):
    return lax.dot_general(a, b, (((1,), (1,)), ((), ())), preferred_element_type=F32)


def _layer_norm(y, g, b):
    mu = jnp.mean(y, axis=-1, keepdims=True)
    d = y - mu
    var = jnp.mean(d * d, axis=-1, keepdims=True)
    return d * lax.rsqrt(var + LN_EPS) * g + b


def _gelu(z):
    return 0.5 * z * (1.0 + lax.erf(z * (2.0 ** -0.5)))


def _even_inproj_kernel(x_ref, wqkv_ref, wf_ref, wc_ref, bf_ref, cw_ref, st_ref,
                        q_ref, kf_ref, vf_ref, kb_ref, vb_ref, lf_ref, gc_ref, cs_ref,
                        prev_ref, *, tm, nt):
    t = pl.program_id(1)

    @pl.when(t == 0)
    def _():
        prev_ref[...] = st_ref[0]

    xb = x_ref[0].astype(BF16)
    q = _dot(xb, wqkv_ref[:, 0:FOX_DIM])
    q_ref[0] = (q * (FOX_HEAD_DIM ** -0.5)).astype(BF16)
    k = _dot(xb, wqkv_ref[:, FOX_DIM:2 * FOX_DIM])
    kf_ref[0] = k
    kb_ref[0] = k.astype(BF16)
    v = _dot(xb, wqkv_ref[:, 2 * FOX_DIM:3 * FOX_DIM])
    vf_ref[0] = v
    vb_ref[0] = v.astype(BF16)

    f = _dot(xb, wf_ref[...]) + bf_ref[...]
    lf_ref[0] = jax.nn.log_sigmoid(f)[:, 0:FOX_HEADS]

    h = _dot(xb, wc_ref[:, 0:CONV_DIM])
    bg = _dot(xb, wc_ref[:, CONV_DIM:2 * CONV_DIM])
    cg = _dot(xb, wc_ref[:, 2 * CONV_DIM:3 * CONV_DIM])
    pre = cg * h

    prev = prev_ref[...]
    row = lax.broadcasted_iota(jnp.int32, (tm, CONV_DIM), 0)
    p1 = jnp.where(row == 0, prev[7:8, :], pltpu.roll(pre, 1, 0))
    p2 = jnp.where(row == 0, prev[6:7, :],
                   jnp.where(row == 1, prev[7:8, :], pltpu.roll(pre, 2, 0)))
    cw = cw_ref[...]
    conv = cw[0:1, :] * p2 + cw[1:2, :] * p1 + cw[2:3, :] * pre
    gc_ref[0] = (bg * conv).astype(BF16)
    prev_ref[...] = pre[tm - SUBLANES:tm, :]

    @pl.when(t == nt - 1)
    def _():
        cs_ref[0] = pre[tm - (CONV_WIDTH - 1):tm, :]


def _even_inproj(x, wqkv, wf, wc, bf, cw, state8, tm):
    bn, tlen, _ = x.shape
    nt = tlen // tm
    tok = lambda w: pl.BlockSpec((1, tm, w), lambda b, t: (b, t, 0))
    out_shape = (
        jax.ShapeDtypeStruct((bn, tlen, FOX_DIM), BF16),
        jax.ShapeDtypeStruct((bn, tlen, FOX_DIM), F32),
        jax.ShapeDtypeStruct((bn, tlen, FOX_DIM), F32),
        jax.ShapeDtypeStruct((bn, tlen, FOX_DIM), BF16),
        jax.ShapeDtypeStruct((bn, tlen, FOX_DIM), BF16),
        jax.ShapeDtypeStruct((bn, tlen, FOX_HEADS), F32),
        jax.ShapeDtypeStruct((bn, tlen, CONV_DIM), BF16),
        jax.ShapeDtypeStruct((bn, CONV_WIDTH - 1, CONV_DIM), F32),
    )
    return pl.pallas_call(
        functools.partial(_even_inproj_kernel, tm=tm, nt=nt),
        grid=(bn, nt),
        in_specs=[tok(D_MODEL), _resident(wqkv.shape), _resident(wf.shape),
                  _resident(wc.shape), _resident(bf.shape), _resident(cw.shape),
                  pl.BlockSpec((1, SUBLANES, CONV_DIM), lambda b, t: (b, 0, 0))],
        out_specs=(tok(FOX_DIM), tok(FOX_DIM), tok(FOX_DIM), tok(FOX_DIM), tok(FOX_DIM),
                   tok(FOX_HEADS), tok(CONV_DIM),
                   pl.BlockSpec((1, CONV_WIDTH - 1, CONV_DIM), lambda b, t: (b, 0, 0))),
        out_shape=out_shape,
        scratch_shapes=[pltpu.VMEM((SUBLANES, CONV_DIM), F32)],
        compiler_params=_params("parallel", "arbitrary"),
        name="even_inproj",
    )(x, wqkv, wf, wc, bf, cw, state8)


def _cumsum_kernel(x_ref, o_ref, *, n):
    lane = lax.broadcasted_iota(jnp.int32, (FOX_HEADS, LANES), 1)
    carry = jnp.zeros((FOX_HEADS, 1), F32)
    for j in range(n // LANES):
        seg = x_ref[0, :, j * LANES:(j + 1) * LANES]
        step = 1
        while step < LANES:
            seg = seg + jnp.where(lane >= step, pltpu.roll(seg, step, 1), 0.0)
            step *= 2
        seg = seg + carry
        o_ref[0, :, j * LANES:(j + 1) * LANES] = seg
        carry = seg[:, LANES - 1:LANES]


def _cumsum_time(lt):
    bn, h, n = lt.shape
    spec = pl.BlockSpec((1, h, n), lambda b: (b, 0, 0))
    return pl.pallas_call(
        functools.partial(_cumsum_kernel, n=n),
        grid=(bn,), in_specs=[spec], out_specs=spec,
        out_shape=jax.ShapeDtypeStruct((bn, h, n), F32),
        compiler_params=_params("parallel"),
        name="logf_cumsum",
    )(lt)


def _head_mask(shape, hh):
    lane = lax.broadcasted_iota(jnp.int32, shape, 1)
    return (lane >= FOX_HEAD_DIM) if hh == 1 else (lane < FOX_HEAD_DIM)


def _fox_prompt_kernel(q_ref, k_ref, v_ref, c_ref, o_ref, *, tq, nq):
    hp = pl.program_id(1)
    i = pl.program_id(2)
    q = q_ref[0]
    row = lax.broadcasted_iota(jnp.int32, (tq, tq), 0)
    col = lax.broadcasted_iota(jnp.int32, (tq, tq), 1)
    outs = []
    for hh in range(2):
        head = hp * 2 + hh
        qh = jnp.where(_head_mask(q.shape, hh), q, jnp.zeros_like(q))
        c_q0 = c_ref[0, head * nq + i][:, 0:1]

        def scores(j):
            start = pl.multiple_of(j * tq, tq)
            kj = k_ref[0, pl.ds(start, tq), :]
            bias = c_q0 - c_ref[0, head * nq + j]
            return _dot_nt(qh, kj) + bias, v_ref[0, pl.ds(start, tq), :]

        def update(carry, s, vj):
            m, l, acc = carry
            m_new = jnp.maximum(m, jnp.max(s, axis=-1, keepdims=True))
            scale = jnp.exp(m - m_new)
            p = jnp.exp(s - m_new)
            l = scale * l + jnp.sum(p, axis=-1, keepdims=True)
            acc = scale * acc + _dot(p.astype(BF16), vj)
            return m_new, l, acc

        def body(j, carry):
            s, vj = scores(j)
            return update(carry, s, vj)

        init = (jnp.full((tq, 1), -jnp.inf, F32), jnp.zeros((tq, 1), F32),
                jnp.zeros((tq, LANES), F32))
        carry = lax.fori_loop(0, i, body, init)
        s, vj = scores(i)
        s = jnp.where(col <= row, s, -jnp.inf)
        _, l, acc = update(carry, s, vj)
        outs.append(acc / l)
    o_ref[0] = jnp.where(_head_mask(outs[0].shape, 0), outs[0], outs[1]).astype(o_ref.dtype)


def _fox_prompt(q, k, v, c, tq):
    bn, s_len, _ = q.shape
    nq = s_len // tq
    c4 = c.reshape(bn, FOX_HEADS * nq, 1, tq)
    qspec = pl.BlockSpec((1, tq, LANES), lambda b, h, i: (b, i, h))
    kspec = pl.BlockSpec((1, s_len, LANES), lambda b, h, i: (b, 0, h))
    return pl.pallas_call(
        functools.partial(_fox_prompt_kernel, tq=tq, nq=nq),
        grid=(bn, FOX_DIM // LANES, nq),
        in_specs=[qspec, kspec, kspec,
                  pl.BlockSpec((1, FOX_HEADS * nq, 1, tq), lambda b, h, i: (b, 0, 0, 0))],
        out_specs=qspec,
        out_shape=jax.ShapeDtypeStruct((bn, s_len, FOX_DIM), BF16),
        compiler_params=_params("parallel", "parallel", "arbitrary"),
        name="fox_prompt",
    )(q, k, v, c4)


def _fox_sample_kernel(q_ref, k_ref, v_ref, c_ref, o_ref, *, t_new, p_len):
    hp = pl.program_id(1)
    q = q_ref[0]
    k = k_ref[0]
    v = v_ref[0]
    n_keys = k.shape[0]
    row = lax.broadcasted_iota(jnp.int32, (t_new, n_keys), 0)
    col = lax.broadcasted_iota(jnp.int32, (t_new, n_keys), 1)
    outs = []
    for hh in range(2):
        head = hp * 2 + hh
        qh = jnp.where(_head_mask(q.shape, hh), q, jnp.zeros_like(q))
        c_row = c_ref[0, head]
        bias = c_row[:, p_len - 1:p_len] - c_row
        s = _dot_nt(qh, k) + bias
        s = jnp.where(col <= row + p_len, s, -jnp.inf)
        m = jnp.max(s, axis=-1, keepdims=True)
        p = jnp.exp(s - m)
        l = jnp.sum(p, axis=-1, keepdims=True)
        outs.append(_dot(p.astype(BF16), v) / l)
    o_ref[0] = jnp.where(_head_mask(outs[0].shape, 0), outs[0], outs[1]).astype(o_ref.dtype)


def _fox_sample(q, k_all, v_all, c, p_len):
    bn, t_new, _ = q.shape
    n_keys = k_all.shape[1]
    qspec = pl.BlockSpec((1, t_new, LANES), lambda b, h: (b, 0, h))
    kspec = pl.BlockSpec((1, n_keys, LANES), lambda b, h: (b, 0, h))
    return pl.pallas_call(
        functools.partial(_fox_sample_kernel, t_new=t_new, p_len=p_len),
        grid=(bn, FOX_DIM // LANES),
        in_specs=[qspec, kspec, kspec,
                  pl.BlockSpec((1, FOX_HEADS, 1, n_keys), lambda b, h: (b, 0, 0, 0))],
        out_specs=qspec,
        out_shape=jax.ShapeDtypeStruct((bn, t_new, FOX_DIM), BF16),
        compiler_params=_params("parallel", "parallel"),
        name="fox_sample",
    )(q, k_all, v_all, c.reshape(bn, FOX_HEADS, 1, n_keys))


def _even_outproj_kernel(x_ref, att_ref, gc_ref, w_ref, g_ref, b_ref, y_ref, *, alpha):
    mix = _dot(att_ref[...], w_ref[0:FOX_DIM, :]) + _dot(gc_ref[...], w_ref[FOX_DIM:, :])
    y_ref[...] = _layer_norm(alpha * x_ref[...] + mix, g_ref[...], b_ref[...])


def _even_outproj(x, att, gc, w, g, b, alpha, tm):
    n = x.shape[0]
    tok = lambda w_: pl.BlockSpec((tm, w_), lambda i: (i, 0))
    return pl.pallas_call(
        functools.partial(_even_outproj_kernel, alpha=alpha),
        grid=(n // tm,),
        in_specs=[tok(D_MODEL), tok(FOX_DIM), tok(CONV_DIM), _resident(w.shape),
                  _resident(g.shape), _resident(b.shape)],
        out_specs=tok(D_MODEL),
        out_shape=jax.ShapeDtypeStruct((n, D_MODEL), F32),
        compiler_params=_params("parallel"),
        name="even_outproj",
    )(x, att, gc, w, g, b)


def _gmlp_kernel(x_ref, win_ref, bin_ref, ng_ref, nb_ref, ws_ref, bst_ref, wout_ref,
                 g_ref, b_ref, y_ref, *rest, tm, alpha, emit_vn):
    if emit_vn:
        vn_ref, u_scr, vnb_scr, gated_scr = rest
    else:
        u_scr, vnb_scr, gated_scr = rest
    x = x_ref[...]
    xb = x.astype(BF16)
    u_scr[...] = _gelu(_dot(xb, win_ref[:, 0:GMLP_DIM]) + bin_ref[:, 0:GMLP_DIM])
    v = _gelu(_dot(xb, win_ref[:, GMLP_DIM:]) + bin_ref[:, GMLP_DIM:])
    vn = _layer_norm(v, ng_ref[...], nb_ref[...])
    if emit_vn:
        vn_ref[...] = vn
    vnb_scr[...] = vn.astype(BF16)

    row = lax.broadcasted_iota(jnp.int32, (GMLP_CHUNK, GMLP_CHUNK), 0)
    col = lax.broadcasted_iota(jnp.int32, (GMLP_CHUNK, GMLP_CHUNK), 1)
    gd = GMLP_DIM // GMLP_GROUPS
    for g in range(GMLP_GROUPS):
        wm = jnp.where(col <= row, ws_ref[g], 0.0).astype(BF16)
        bias = bst_ref[:, g:g + 1]
        for r in range(tm // GMLP_CHUNK):
            rs = slice(r * GMLP_CHUNK, (r + 1) * GMLP_CHUNK)
            cs = slice(g * gd, (g + 1) * gd)
            mixed = _dot(wm, vnb_scr[rs, cs]) + bias
            gated_scr[rs, cs] = (u_scr[rs, cs] * mixed).astype(BF16)
    out = _dot(gated_scr[...], wout_ref[...])
    y_ref[...] = _layer_norm(alpha * x + out, g_ref[...], b_ref[...])


def _gmlp_mixer(x, win, bin_, ng, nb, ws, bst, wout, g, b, alpha, tm, emit_vn):
    n = x.shape[0]
    tok = pl.BlockSpec((tm, D_MODEL), lambda i: (i, 0))
    out_shape = [jax.ShapeDtypeStruct((n, D_MODEL), F32)]
    out_specs = [tok]
    if emit_vn:
        out_shape.append(jax.ShapeDtypeStruct((n, GMLP_DIM), F32))
        out_specs.append(tok)
    res = pl.pallas_call(
        functools.partial(_gmlp_kernel, tm=tm, alpha=alpha, emit_vn=emit_vn),
        grid=(n // tm,),
        in_specs=[tok] + [_resident(a.shape) for a in (win, bin_, ng, nb, ws, bst, wout, g, b)],
        out_specs=tuple(out_specs),
        out_shape=tuple(out_shape),
        scratch_shapes=[pltpu.VMEM((tm, GMLP_DIM), F32), pltpu.VMEM((tm, GMLP_DIM), BF16),
                        pltpu.VMEM((tm, GMLP_DIM), BF16)],
        compiler_params=_params("parallel"),
        name="gmlp_mixer",
    )(x, win, bin_, ng, nb, ws, bst, wout, g, b)
    return res if emit_vn else (res[0], None)


def _mem_kv_kernel(m_ref, wk_ref, wv_ref, k_ref, v_ref):
    mb = m_ref[...].astype(BF16)
    k_ref[0] = _dot(mb, wk_ref[0])
    v_ref[0] = _dot(mb, wv_ref[0])


def _mem_kv(mem, wk, wv, tm):
    n = mem.shape[0]
    depth = wk.shape[0]
    wspec = pl.BlockSpec((1, D_MODEL, MEM_DIM), lambda l, i: (l, 0, 0))
    ospec = pl.BlockSpec((1, tm, MEM_DIM), lambda l, i: (l, i, 0))
    shape = jax.ShapeDtypeStruct((depth, n, MEM_DIM), F32)
    return pl.pallas_call(
        _mem_kv_kernel,
        grid=(depth, n // tm),
        in_specs=[pl.BlockSpec((tm, D_MODEL), lambda l, i: (i, 0)), wspec, wspec],
        out_specs=(ospec, ospec), out_shape=(shape, shape),
        compiler_params=_params("parallel", "parallel"),
        name="mem_kv",
    )(mem, wk, wv)


def _mem_attn_kernel(x_ref, mk_ref, mv_ref, wq_ref, wo_ref, g_ref, b_ref, y_ref, *, alpha):
    x = x_ref[0]
    q = _dot(x.astype(BF16), wq_ref[...]) * (MEM_HEAD_DIM ** -0.5)
    qb = q.astype(BF16)
    mk = mk_ref[0].astype(BF16)
    mv = mv_ref[0].astype(BF16)
    heads = []
    for h in range(MEM_HEADS):
        hs = slice(h * MEM_HEAD_DIM, (h + 1) * MEM_HEAD_DIM)
        s = _dot_nt(qb[:, hs], mk[:, hs])
        m = jnp.max(s, axis=-1, keepdims=True)
        p = jnp.exp(s - m)
        l = jnp.sum(p, axis=-1, keepdims=True)
        heads.append((_dot(p.astype(BF16), mv[:, hs]) / l).astype(BF16))
    o = jnp.concatenate(heads, axis=-1)
    y_ref[0] = _layer_norm(alpha * x + _dot(o, wo_ref[...]), g_ref[...], b_ref[...])


def _mem_attn(x, mk, mv, wq, wo, g, b, alpha, tm):
    bn, tlen, _ = x.shape
    mlen = mk.shape[1]
    tok = pl.BlockSpec((1, tm, D_MODEL), lambda bb, t: (bb, t, 0))
    mspec = pl.BlockSpec((1, mlen, MEM_DIM), lambda bb, t: (bb, 0, 0))
    return pl.pallas_call(
        functools.partial(_mem_attn_kernel, alpha=alpha),
        grid=(bn, tlen // tm),
        in_specs=[tok, mspec, mspec, _resident(wq.shape), _resident(wo.shape),
                  _resident(g.shape), _resident(b.shape)],
        out_specs=tok,
        out_shape=jax.ShapeDtypeStruct((bn, tlen, D_MODEL), F32),
        compiler_params=_params("parallel", "parallel"),
        name="mem_attn",
    )(x, mk, mv, wq, wo, g, b)


def _ffn_kernel(x_ref, wg_ref, wu_ref, wd_ref, g_ref, b_ref, y_ref, h_scr, *, alpha, nck):
    x = x_ref[...]
    xb = x.astype(BF16)
    for c in range(nck):
        gate = _dot(xb, wg_ref[c])
        up = _dot(xb, wu_ref[c])
        h_scr[:, c * FFN_CHUNK:(c + 1) * FFN_CHUNK] = (gate * jax.nn.sigmoid(gate) * up).astype(BF16)
    y = alpha * x + _dot(h_scr[...], wd_ref[...])
    y_ref[...] = _layer_norm(y, g_ref[...], b_ref[...])


def _ffn(x, wg, wu, wd, g, b, alpha, tm):
    n = x.shape[0]
    nck = wg.shape[0]
    tok = pl.BlockSpec((tm, D_MODEL), lambda i: (i, 0))
    return pl.pallas_call(
        functools.partial(_ffn_kernel, alpha=alpha, nck=nck),
        grid=(n // tm,),
        in_specs=[tok] + [_resident(a.shape) for a in (wg, wu, wd, g, b)],
        out_specs=tok,
        out_shape=jax.ShapeDtypeStruct((n, D_MODEL), F32),
        scratch_shapes=[pltpu.VMEM((tm, nck * FFN_CHUNK), BF16)],
        compiler_params=_params("parallel"),
        name="swiglu_ffn",
    )(x, wg, wu, wd, g, b)


def _chunk_cols(w, chunk):
    k, n = w.shape
    return w.reshape(k, n // chunk, chunk).transpose(1, 0, 2)


def kernel(x_prompt, x_sample, cache_fox_k, cache_fox_v, cache_fox_logf, state_conv, cache_mem_k, cache_mem_v, mem_prompt, w_in_even, b_forget, conv_w, w_out_even, w_in_odd, b_in_odd, gmlp_norm_g, gmlp_norm_b, gmlp_w_s, gmlp_b_s, w_out_odd, mem_w_q, mem_w_k, mem_w_v, mem_w_o, ffn_w_gate, ffn_w_up, ffn_w_down, ln_g, ln_b):
    depth = ln_g.shape[0]
    alpha = float((2 * depth) ** 0.25)
    bp, s_len, d = x_prompt.shape
    bs, t_new, _ = x_sample.shape
    p_len = cache_fox_k.shape[2]
    mem_len = mem_prompt.shape[1]
    d_ff = ffn_w_gate.shape[-1]
    row2 = lambda a: a.reshape(1, -1)

    mk_all, mv_all = _mem_kv(mem_prompt.reshape(bp * mem_len, d), mem_w_k.astype(BF16),
                             mem_w_v.astype(BF16), TOKEN_TILE)

    yp, ys = x_prompt, x_sample
    fk_p, fv_p, fl_p, cs_p = [], [], [], []
    fk_s, fv_s, fl_s, cs_s, gv_s = [], [], [], [], []
    for layer in range(depth):
        lg, lb = ln_g[layer], ln_b[layer]
        if layer % 2 == 0:
            e = layer // 2
            w_in = w_in_even[e].astype(BF16)
            qkv_w = w_in[:, 0:3 * FOX_DIM]
            f_w = jnp.pad(w_in[:, 3 * FOX_DIM:3 * FOX_DIM + FOX_HEADS], ((0, 0), (0, LANES - FOX_HEADS)))
            c_w = w_in[:, 3 * FOX_DIM + FOX_HEADS:]
            f_b = jnp.pad(b_forget[e], (0, LANES - FOX_HEADS)).reshape(1, LANES)
            w_out = w_out_even[e].astype(BF16)
            cw = conv_w[e]

            st0 = jnp.zeros((bp, SUBLANES, CONV_DIM), F32)
            q, kf, vf, kb, vb, lf, gc, cst = _even_inproj(yp, qkv_w, f_w, c_w, f_b, cw, st0, TOKEN_TILE)
            c = _cumsum_time(jnp.swapaxes(lf, 1, 2))
            att = _fox_prompt(q, kb, vb, c, FOX_TILE)
            yp = _even_outproj(yp.reshape(-1, d), att.reshape(-1, FOX_DIM), gc.reshape(-1, CONV_DIM),
                               w_out, row2(lg[0]), row2(lb[0]), alpha, TOKEN_TILE).reshape(bp, s_len, d)
            fk_p.append(kf.reshape(bp, s_len, FOX_HEADS, FOX_HEAD_DIM))
            fv_p.append(vf.reshape(bp, s_len, FOX_HEADS, FOX_HEAD_DIM))
            fl_p.append(lf)
            cs_p.append(cst)

            st = jnp.pad(state_conv[e], ((0, 0), (SUBLANES - (CONV_WIDTH - 1), 0), (0, 0)))
            q, kf, vf, kb, vb, lf, gc, cst = _even_inproj(ys, qkv_w, f_w, c_w, f_b, cw, st, t_new)
            n_keys = -(-(p_len + t_new) // LANES) * LANES
            pad = n_keys - p_len - t_new
            k_all = jnp.concatenate([cache_fox_k[e].reshape(bs, p_len, FOX_DIM).astype(BF16), kb,
                                     jnp.zeros((bs, pad, FOX_DIM), BF16)], axis=1)
            v_all = jnp.concatenate([cache_fox_v[e].reshape(bs, p_len, FOX_DIM).astype(BF16), vb,
                                     jnp.zeros((bs, pad, FOX_DIM), BF16)], axis=1)
            lt = jnp.concatenate([jnp.swapaxes(cache_fox_logf[e], 1, 2), jnp.swapaxes(lf, 1, 2),
                                  jnp.zeros((bs, FOX_HEADS, pad), F32)], axis=2)
            att = _fox_sample(q, k_all, v_all, _cumsum_time(lt), p_len)
            ys = _even_outproj(ys.reshape(-1, d), att.reshape(-1, FOX_DIM), gc.reshape(-1, CONV_DIM),
                               w_out, row2(lg[0]), row2(lb[0]), alpha, bs * t_new).reshape(bs, t_new, d)
            fk_s.append(kf.reshape(bs, t_new, FOX_HEADS, FOX_HEAD_DIM))
            fv_s.append(vf.reshape(bs, t_new, FOX_HEADS, FOX_HEAD_DIM))
            fl_s.append(lf)
            cs_s.append(cst)
        else:
            o = layer // 2
            args = (w_in_odd[o].astype(BF16), row2(b_in_odd[o]), row2(gmlp_norm_g[o]), row2(gmlp_norm_b[o]),
                    gmlp_w_s[o], gmlp_b_s[o].T, w_out_odd[o].astype(BF16), row2(lg[0]), row2(lb[0]), alpha)
            yp, _ = _gmlp_mixer(yp.reshape(-1, d), *args, TOKEN_TILE, False)
            yp = yp.reshape(bp, s_len, d)
            ys_pad = jnp.pad(ys, ((0, 0), (0, GMLP_CHUNK - t_new), (0, 0))).reshape(-1, d)
            ys_pad, vn = _gmlp_mixer(ys_pad, *args, TOKEN_TILE, True)
            ys = ys_pad.reshape(bs, GMLP_CHUNK, d)[:, :t_new]
            gv_s.append(vn.reshape(bs, GMLP_CHUNK, GMLP_DIM)[:, :t_new])

        wq, wo = mem_w_q[layer].astype(BF16), mem_w_o[layer].astype(BF16)
        yp = _mem_attn(yp, mk_all[layer].reshape(bp, mem_len, MEM_DIM), mv_all[layer].reshape(bp, mem_len, MEM_DIM),
                       wq, wo, row2(lg[1]), row2(lb[1]), alpha, TOKEN_TILE)
        ys = _mem_attn(ys, cache_mem_k[layer].reshape(bs, mem_len, MEM_DIM),
                       cache_mem_v[layer].reshape(bs, mem_len, MEM_DIM),
                       wq, wo, row2(lg[1]), row2(lb[1]), alpha, t_new)

        wg = _chunk_cols(ffn_w_gate[layer].astype(BF16), FFN_CHUNK)
        wu = _chunk_cols(ffn_w_up[layer].astype(BF16), FFN_CHUNK)
        wd = ffn_w_down[layer].astype(BF16)
        yp = _ffn(yp.reshape(-1, d), wg, wu, wd, row2(lg[2]), row2(lb[2]), alpha, TOKEN_TILE).reshape(bp, s_len, d)
        ys = _ffn(ys.reshape(-1, d), wg, wu, wd, row2(lg[2]), row2(lb[2]), alpha, bs * t_new).reshape(bs, t_new, d)

    mshape = (depth, bp, mem_len, MEM_HEADS, MEM_HEAD_DIM)
    return (yp, ys, jnp.stack(fk_p), jnp.stack(fv_p), jnp.stack(fl_p), jnp.stack(cs_p),
            mk_all.reshape(mshape), mv_all.reshape(mshape),
            jnp.stack(fk_s), jnp.stack(fv_s), jnp.stack(fl_s), jnp.stack(cs_s), jnp.stack(gv_s))
```

```python
import functools

import jax
import jax.numpy as jnp
from jax import lax
from jax.experimental import pallas as pl
from jax.experimental.pallas import tpu as pltpu

F32 = jnp.float32
BF16 = jnp.bfloat16

D_MODEL = 1024
FOX_HEADS = 8
FOX_HEAD_DIM = 64
FOX_DIM = FOX_HEADS * FOX_HEAD_DIM
CONV_DIM = 512
CONV_WIDTH = 3
GMLP_CHUNK = 128
GMLP_GROUPS = 8
GMLP_DIM = D_MODEL
MEM_HEADS = 4
MEM_HEAD_DIM = 128
MEM_DIM = MEM_HEADS * MEM_HEAD_DIM
LN_EPS = 1e-5

LANES = 128
SUBLANES = 8
VMEM_LIMIT_BYTES = 56 * 1024 * 1024
TOKEN_TILE = 512
FOX_TILE = 512
FFN_CHUNK = 256
LOG2E = 1.4426950408889634


def _params(*semantics):
    return pltpu.CompilerParams(dimension_semantics=semantics,
                                vmem_limit_bytes=VMEM_LIMIT_BYTES)


def _resident(shape):
    nd = len(shape)
    return pl.BlockSpec(shape, lambda *_: (0,) * nd, pipeline_mode=pl.Buffered(1))


def _dot(a, b):
    return jnp.dot(a, b, preferred_element_type=F32)


def _dot_nt(a, b):
    return lax.dot_general(a, b, (((1,), (1,)), ((), ())), preferred_element_type=F32)


def _layer_norm(y, g, b):
    mu = jnp.mean(y, axis=-1, keepdims=True)
    d = y - mu
    var = jnp.mean(d * d, axis=-1, keepdims=True)
    return d * lax.rsqrt(var + LN_EPS) * g + b


def _gelu(z):
    return 0.5 * z * (1.0 + lax.erf(z * (2.0 ** -0.5)))


def _even_inproj_kernel(x_ref, wqkv_ref, wf_ref, wc_ref, bf_ref, cw_ref, st_ref,
                        q_ref, kf_ref, vf_ref, kb_ref, vb_ref, lf_ref, gc_ref, cs_ref,
                        prev_ref, *, tm, nt):
    t = pl.program_id(1)

    @pl.when(t == 0)
    def _():
        prev_ref[...] = st_ref[0]

    xb = x_ref[0].astype(BF16)
    q = _dot(xb, wqkv_ref[:, 0:FOX_DIM])
    q_ref[0] = (q * (LOG2E * FOX_HEAD_DIM ** -0.5)).astype(BF16)
    k = _dot(xb, wqkv_ref[:, FOX_DIM:2 * FOX_DIM])
    kf_ref[0] = k
    kb_ref[0] = k.astype(BF16)
    v = _dot(xb, wqkv_ref[:, 2 * FOX_DIM:3 * FOX_DIM])
    vf_ref[0] = v
    vb_ref[0] = v.astype(BF16)

    f = _dot(xb, wf_ref[...]) + bf_ref[...]
    lf_ref[0] = jax.nn.log_sigmoid(f)[:, 0:FOX_HEADS]

    h = _dot(xb, wc_ref[:, 0:CONV_DIM])
    bg = _dot(xb, wc_ref[:, CONV_DIM:2 * CONV_DIM])
    cg = _dot(xb, wc_ref[:, 2 * CONV_DIM:3 * CONV_DIM])
    pre = cg * h

    prev = prev_ref[...]
    row = lax.broadcasted_iota(jnp.int32, (tm, CONV_DIM), 0)
    p1 = jnp.where(row == 0, prev[7:8, :], pltpu.roll(pre, 1, 0))
    p2 = jnp.where(row == 0, prev[6:7, :],
                   jnp.where(row == 1, prev[7:8, :], pltpu.roll(pre, 2, 0)))
    cw = cw_ref[...]
    conv = cw[0:1, :] * p2 + cw[1:2, :] * p1 + cw[2:3, :] * pre
    gc_ref[0] = (bg * conv).astype(BF16)
    prev_ref[...] = pre[tm - SUBLANES:tm, :]

    @pl.when(t == nt - 1)
    def _():
        cs_ref[0] = pre[tm - (CONV_WIDTH - 1):tm, :]


def _even_inproj(x, wqkv, wf, wc, bf, cw, state8, tm):
    bn, tlen, _ = x.shape
    nt = tlen // tm
    tok = lambda w: pl.BlockSpec((1, tm, w), lambda b, t: (b, t, 0))
    out_shape = (
        jax.ShapeDtypeStruct((bn, tlen, FOX_DIM), BF16),
        jax.ShapeDtypeStruct((bn, tlen, FOX_DIM), F32),
        jax.ShapeDtypeStruct((bn, tlen, FOX_DIM), F32),
        jax.ShapeDtypeStruct((bn, tlen, FOX_DIM), BF16),
        jax.ShapeDtypeStruct((bn, tlen, FOX_DIM), BF16),
        jax.ShapeDtypeStruct((bn, tlen, FOX_HEADS), F32),
        jax.ShapeDtypeStruct((bn, tlen, CONV_DIM), BF16),
        jax.ShapeDtypeStruct((bn, CONV_WIDTH - 1, CONV_DIM), F32),
    )
    return pl.pallas_call(
        functools.partial(_even_inproj_kernel, tm=tm, nt=nt),
        grid=(bn, nt),
        in_specs=[tok(D_MODEL), _resident(wqkv.shape), _resident(wf.shape),
                  _resident(wc.shape), _resident(bf.shape), _resident(cw.shape),
                  pl.BlockSpec((1, SUBLANES, CONV_DIM), lambda b, t: (b, 0, 0))],
        out_specs=(tok(FOX_DIM), tok(FOX_DIM), tok(FOX_DIM), tok(FOX_DIM), tok(FOX_DIM),
                   tok(FOX_HEADS), tok(CONV_DIM),
                   pl.BlockSpec((1, CONV_WIDTH - 1, CONV_DIM), lambda b, t: (b, 0, 0))),
        out_shape=out_shape,
        scratch_shapes=[pltpu.VMEM((SUBLANES, CONV_DIM), F32)],
        compiler_params=_params("parallel", "arbitrary"),
        name="even_inproj",
    )(x, wqkv, wf, wc, bf, cw, state8)


def _cumsum_kernel(x_ref, o_ref, *, n):
    lane = lax.broadcasted_iota(jnp.int32, (FOX_HEADS, LANES), 1)
    carry = jnp.zeros((FOX_HEADS, 1), F32)
    for j in range(n // LANES):
        seg = x_ref[0, :, j * LANES:(j + 1) * LANES]
        step = 1
        while step < LANES:
            seg = seg + jnp.where(lane >= step, pltpu.roll(seg, step, 1), 0.0)
            step *= 2
        seg = seg + carry
        o_ref[0, :, j * LANES:(j + 1) * LANES] = seg
        carry = seg[:, LANES - 1:LANES]


def _cumsum_time(lt):
    bn, h, n = lt.shape
    spec = pl.BlockSpec((1, h, n), lambda b: (b, 0, 0))
    return pl.pallas_call(
        functools.partial(_cumsum_kernel, n=n),
        grid=(bn,), in_specs=[spec], out_specs=spec,
        out_shape=jax.ShapeDtypeStruct((bn, h, n), F32),
        compiler_params=_params("parallel"),
        name="logf_cumsum",
    )(lt)


def _head_mask(shape, hh):
    lane = lax.broadcasted_iota(jnp.int32, shape, 1)
    return (lane >= FOX_HEAD_DIM) if hh == 1 else (lane < FOX_HEAD_DIM)


def _fox_prompt_kernel(q_ref, k_ref, v_ref, c_ref, o_ref, vaug_ref, *, tq, nq):
    hp = pl.program_id(1)
    i = pl.program_id(2)

    @pl.when(i == 0)
    def _():
        v = v_ref[0]
        for hh in range(2):
            vaug_ref[hh] = jnp.where(_head_mask(v.shape, hh), v, jnp.ones_like(v))

    q = q_ref[0]
    qh = [jnp.where(_head_mask(q.shape, hh), q, jnp.zeros_like(q)) for hh in range(2)]
    c_q0 = [c_ref[0, (hp * 2 + hh) * nq + i][:, 0:1] for hh in range(2)]
    row = lax.broadcasted_iota(jnp.int32, (tq, tq), 0)
    col = lax.broadcasted_iota(jnp.int32, (tq, tq), 1)

    def scores(j, hh):
        kj = k_ref[0, pl.ds(pl.multiple_of(j * tq, tq), tq), :]
        bias = (c_q0[hh] - c_ref[0, (hp * 2 + hh) * nq + j]) * LOG2E
        return _dot_nt(qh[hh], kj) + bias

    def pv(p, j, hh):
        return _dot(p, vaug_ref[hh, pl.ds(pl.multiple_of(j * tq, tq), tq), :])

    state = []
    for hh in range(2):
        s = jnp.where(col <= row, scores(i, hh), -jnp.inf)
        m = jnp.max(s, axis=-1, keepdims=True)
        state.append((m, pv(jnp.exp2(s - m).astype(BF16), i, hh)))

    def body(j, carry):
        ss = [scores(j, hh) for hh in range(2)]
        new = []
        for hh in range(2):
            m, acc = carry[hh]
            m_new = jnp.maximum(m, jnp.max(ss[hh], axis=-1, keepdims=True))
            p = jnp.exp2(ss[hh] - m_new).astype(BF16)
            new.append((m_new, jnp.exp2(m - m_new) * acc + pv(p, j, hh)))
        return tuple(new)

    carry = lax.fori_loop(0, i, body, tuple(state))
    outs = [acc / pltpu.roll(acc, FOX_HEAD_DIM, 1) for _, acc in carry]
    o_ref[0] = jnp.where(_head_mask(outs[0].shape, 0), outs[0], outs[1]).astype(o_ref.dtype)


def _fox_prompt(q, k, v, c, tq):
    bn, s_len, _ = q.shape
    nq = s_len // tq
    c4 = c.reshape(bn, FOX_HEADS * nq, 1, tq)
    qspec = pl.BlockSpec((1, tq, LANES), lambda b, h, i: (b, i, h))
    kspec = pl.BlockSpec((1, s_len, LANES), lambda b, h, i: (b, 0, h))
    return pl.pallas_call(
        functools.partial(_fox_prompt_kernel, tq=tq, nq=nq),
        grid=(bn, FOX_DIM // LANES, nq),
        in_specs=[qspec, kspec, kspec,
                  pl.BlockSpec((1, FOX_HEADS * nq, 1, tq), lambda b, h, i: (b, 0, 0, 0))],
        out_specs=qspec,
        out_shape=jax.ShapeDtypeStruct((bn, s_len, FOX_DIM), BF16),
        scratch_shapes=[pltpu.VMEM((2, s_len, LANES), BF16)],
        compiler_params=_params("parallel", "parallel", "arbitrary"),
        name="fox_prompt",
    )(q, k, v, c4)


def _fox_sample_kernel(q_ref, kn_ref, vn_ref, kc_ref, vc_ref, c_ref, o_ref, *, t_new, p_len):
    hp = pl.program_id(1)
    q = q_ref[0]
    n_new = c_ref.shape[-1] - p_len
    zeros = jnp.zeros((n_new - t_new, LANES), BF16)
    kn = jnp.concatenate([kn_ref[0], zeros], axis=0)
    vn = jnp.concatenate([vn_ref[0], zeros], axis=0)
    kc = kc_ref[0].astype(BF16)
    vc = vc_ref[0].astype(BF16)
    row = lax.broadcasted_iota(jnp.int32, (t_new, n_new), 0)
    col = lax.broadcasted_iota(jnp.int32, (t_new, n_new), 1)
    outs = []
    for hh in range(2):
        head = hp * 2 + hh
        qh = jnp.where(_head_mask(q.shape, hh), q, jnp.zeros_like(q))
        c_row = c_ref[0, head]
        bias = (c_row[:, p_len - 1:p_len] - c_row) * LOG2E
        s_c = _dot_nt(qh, kc) + bias[:, 0:p_len]
        s_n = jnp.where(col <= row, _dot_nt(qh, kn) + bias[:, p_len:], -jnp.inf)
        m = jnp.maximum(jnp.max(s_c, axis=-1, keepdims=True), jnp.max(s_n, axis=-1, keepdims=True))
        p_c = jnp.exp2(s_c - m)
        p_n = jnp.exp2(s_n - m)
        l = jnp.sum(p_c, axis=-1, keepdims=True) + jnp.sum(p_n, axis=-1, keepdims=True)
        outs.append((_dot(p_c.astype(BF16), vc) + _dot(p_n.astype(BF16), vn)) / l)
    o_ref[0] = jnp.where(_head_mask(outs[0].shape, 0), outs[0], outs[1]).astype(o_ref.dtype)


def _fox_sample(q, kn, vn, kc, vc, c):
    bn, t_new, _ = q.shape
    p_len = kc.shape[1]
    n_keys = c.shape[-1]
    qspec = pl.BlockSpec((1, t_new, LANES), lambda b, h: (b, 0, h))
    cspec = pl.BlockSpec((1, p_len, LANES), lambda b, h: (b, 0, h))
    return pl.pallas_call(
        functools.partial(_fox_sample_kernel, t_new=t_new, p_len=p_len),
        grid=(bn, FOX_DIM // LANES),
        in_specs=[qspec, qspec, qspec, cspec, cspec,
                  pl.BlockSpec((1, FOX_HEADS, 1, n_keys), lambda b, h: (b, 0, 0, 0))],
        out_specs=qspec,
        out_shape=jax.ShapeDtypeStruct((bn, t_new, FOX_DIM), BF16),
        compiler_params=_params("parallel", "parallel"),
        name="fox_sample",
    )(q, kn, vn, kc, vc, c.reshape(bn, FOX_HEADS, 1, n_keys))


def _even_outproj_kernel(x_ref, att_ref, gc_ref, w_ref, g_ref, b_ref, y_ref, *, alpha):
    mix = _dot(att_ref[...], w_ref[0:FOX_DIM, :]) + _dot(gc_ref[...], w_ref[FOX_DIM:, :])
    y_ref[...] = _layer_norm(alpha * x_ref[...] + mix, g_ref[...], b_ref[...])


def _even_outproj(x, att, gc, w, g, b, alpha, tm):
    n = x.shape[0]
    tok = lambda w_: pl.BlockSpec((tm, w_), lambda i: (i, 0))
    return pl.pallas_call(
        functools.partial(_even_outproj_kernel, alpha=alpha),
        grid=(n // tm,),
        in_specs=[tok(D_MODEL), tok(FOX_DIM), tok(CONV_DIM), _resident(w.shape),
                  _resident(g.shape), _resident(b.shape)],
        out_specs=tok(D_MODEL),
        out_shape=jax.ShapeDtypeStruct((n, D_MODEL), F32),
        compiler_params=_params("parallel"),
        name="even_outproj",
    )(x, att, gc, w, g, b)


def _gmlp_kernel(x_ref, win_ref, bin_ref, ng_ref, nb_ref, ws_ref, bst_ref, wout_ref,
                 g_ref, b_ref, y_ref, *rest, tm, alpha, emit_vn):
    if emit_vn:
        vn_ref, u_scr, vnb_scr, gated_scr = rest
    else:
        u_scr, vnb_scr, gated_scr = rest
    x = x_ref[...]
    xb = x.astype(BF16)
    u_scr[...] = _gelu(_dot(xb, win_ref[:, 0:GMLP_DIM]) + bin_ref[:, 0:GMLP_DIM])
    v = _gelu(_dot(xb, win_ref[:, GMLP_DIM:]) + bin_ref[:, GMLP_DIM:])
    vn = _layer_norm(v, ng_ref[...], nb_ref[...])
    if emit_vn:
        vn_ref[...] = vn
    vnb_scr[...] = vn.astype(BF16)

    row = lax.broadcasted_iota(jnp.int32, (GMLP_CHUNK, GMLP_CHUNK), 0)
    col = lax.broadcasted_iota(jnp.int32, (GMLP_CHUNK, GMLP_CHUNK), 1)
    gd = GMLP_DIM // GMLP_GROUPS
    for g in range(GMLP_GROUPS):
        wm = jnp.where(col <= row, ws_ref[g], 0.0).astype(BF16)
        bias = bst_ref[:, g:g + 1]
        for r in range(tm // GMLP_CHUNK):
            rs = slice(r * GMLP_CHUNK, (r + 1) * GMLP_CHUNK)
            cs = slice(g * gd, (g + 1) * gd)
            mixed = _dot(wm, vnb_scr[rs, cs]) + bias
            gated_scr[rs, cs] = (u_scr[rs, cs] * mixed).astype(BF16)
    out = _dot(gated_scr[...], wout_ref[...])
    y_ref[...] = _layer_norm(alpha * x + out, g_ref[...], b_ref[...])


def _gmlp_mixer(x, win, bin_, ng, nb, ws, bst, wout, g, b, alpha, tm, emit_vn):
    n = x.shape[0]
    tok = pl.BlockSpec((tm, D_MODEL), lambda i: (i, 0))
    out_shape = [jax.ShapeDtypeStruct((n, D_MODEL), F32)]
    out_specs = [tok]
    if emit_vn:
        out_shape.append(jax.ShapeDtypeStruct((n, GMLP_DIM), F32))
        out_specs.append(tok)
    res = pl.pallas_call(
        functools.partial(_gmlp_kernel, tm=tm, alpha=alpha, emit_vn=emit_vn),
        grid=(n // tm,),
        in_specs=[tok] + [_resident(a.shape) for a in (win, bin_, ng, nb, ws, bst, wout, g, b)],
        out_specs=tuple(out_specs),
        out_shape=tuple(out_shape),
        scratch_shapes=[pltpu.VMEM((tm, GMLP_DIM), F32), pltpu.VMEM((tm, GMLP_DIM), BF16),
                        pltpu.VMEM((tm, GMLP_DIM), BF16)],
        compiler_params=_params("parallel"),
        name="gmlp_mixer",
    )(x, win, bin_, ng, nb, ws, bst, wout, g, b)
    return res if emit_vn else (res[0], None)


def _mem_kv_kernel(m_ref, wk_ref, wv_ref, k_ref, v_ref):
    mb = m_ref[...].astype(BF16)
    k_ref[0] = _dot(mb, wk_ref[0])
    v_ref[0] = _dot(mb, wv_ref[0])


def _mem_kv(mem, wk, wv, tm):
    n = mem.shape[0]
    depth = wk.shape[0]
    wspec = pl.BlockSpec((1, D_MODEL, MEM_DIM), lambda l, i: (l, 0, 0))
    ospec = pl.BlockSpec((1, tm, MEM_DIM), lambda l, i: (l, i, 0))
    shape = jax.ShapeDtypeStruct((depth, n, MEM_DIM), F32)
    return pl.pallas_call(
        _mem_kv_kernel,
        grid=(depth, n // tm),
        in_specs=[pl.BlockSpec((tm, D_MODEL), lambda l, i: (i, 0)), wspec, wspec],
        out_specs=(ospec, ospec), out_shape=(shape, shape),
        compiler_params=_params("parallel", "parallel"),
        name="mem_kv",
    )(mem, wk, wv)


def _mem_attn_kernel(x_ref, mk_ref, mv_ref, wq_ref, wo_ref, g_ref, b_ref, y_ref, *, alpha):
    x = x_ref[0]
    q = _dot(x.astype(BF16), wq_ref[...]) * (MEM_HEAD_DIM ** -0.5)
    qb = q.astype(BF16)
    mk = mk_ref[0].astype(BF16)
    mv = mv_ref[0].astype(BF16)
    heads = []
    for h in range(MEM_HEADS):
        hs = slice(h * MEM_HEAD_DIM, (h + 1) * MEM_HEAD_DIM)
        s = _dot_nt(qb[:, hs], mk[:, hs])
        m = jnp.max(s, axis=-1, keepdims=True)
        p = jnp.exp(s - m)
        l = jnp.sum(p, axis=-1, keepdims=True)
        heads.append((_dot(p.astype(BF16), mv[:, hs]) / l).astype(BF16))
    o = jnp.concatenate(heads, axis=-1)
    y_ref[0] = _layer_norm(alpha * x + _dot(o, wo_ref[...]), g_ref[...], b_ref[...])


def _mem_attn(x, mk, mv, wq, wo, g, b, alpha, tm):
    bn, tlen, _ = x.shape
    mlen = mk.shape[1]
    tok = pl.BlockSpec((1, tm, D_MODEL), lambda bb, t: (bb, t, 0))
    mspec = pl.BlockSpec((1, mlen, MEM_DIM), lambda bb, t: (bb, 0, 0))
    return pl.pallas_call(
        functools.partial(_mem_attn_kernel, alpha=alpha),
        grid=(bn, tlen // tm),
        in_specs=[tok, mspec, mspec, _resident(wq.shape), _resident(wo.shape),
                  _resident(g.shape), _resident(b.shape)],
        out_specs=tok,
        out_shape=jax.ShapeDtypeStruct((bn, tlen, D_MODEL), F32),
        compiler_params=_params("parallel", "parallel"),
        name="mem_attn",
    )(x, mk, mv, wq, wo, g, b)


def _ffn_kernel(x_ref, wg_ref, wu_ref, wd_ref, g_ref, b_ref, y_ref, h_scr, *, alpha):
    x = x_ref[...]
    xb = x.astype(BF16)
    d_ff = wg_ref.shape[1]
    for c in range(d_ff // FFN_CHUNK):
        cols = slice(c * FFN_CHUNK, (c + 1) * FFN_CHUNK)
        gate = _dot(xb, wg_ref[:, cols])
        up = _dot(xb, wu_ref[:, cols])
        h_scr[:, cols] = (gate * jax.nn.sigmoid(gate) * up).astype(BF16)
    y = alpha * x + _dot(h_scr[...], wd_ref[...])
    y_ref[...] = _layer_norm(y, g_ref[...], b_ref[...])


def _ffn(x, wg, wu, wd, g, b, alpha, tm):
    n = x.shape[0]
    d_ff = wg.shape[1]
    assert d_ff % FFN_CHUNK == 0
    tok = pl.BlockSpec((tm, D_MODEL), lambda i: (i, 0))
    return pl.pallas_call(
        functools.partial(_ffn_kernel, alpha=alpha),
        grid=(n // tm,),
        in_specs=[tok] + [_resident(a.shape) for a in (wg, wu, wd, g, b)],
        out_specs=tok,
        out_shape=jax.ShapeDtypeStruct((n, D_MODEL), F32),
        scratch_shapes=[pltpu.VMEM((tm, d_ff), BF16)],
        compiler_params=_params("parallel"),
        name="swiglu_ffn",
    )(x, wg, wu, wd, g, b)


def kernel(x_prompt, x_sample, cache_fox_k, cache_fox_v, cache_fox_logf, state_conv, cache_mem_k, cache_mem_v, mem_prompt, w_in_even, b_forget, conv_w, w_out_even, w_in_odd, b_in_odd, gmlp_norm_g, gmlp_norm_b, gmlp_w_s, gmlp_b_s, w_out_odd, mem_w_q, mem_w_k, mem_w_v, mem_w_o, ffn_w_gate, ffn_w_up, ffn_w_down, ln_g, ln_b):
    depth = ln_g.shape[0]
    alpha = float((2 * depth) ** 0.25)
    bp, s_len, d = x_prompt.shape
    bs, t_new, _ = x_sample.shape
    p_len = cache_fox_k.shape[2]
    mem_len = mem_prompt.shape[1]
    d_ff = ffn_w_gate.shape[-1]
    row2 = lambda a: a.reshape(1, -1)

    mk_all, mv_all = _mem_kv(mem_prompt.reshape(bp * mem_len, d), mem_w_k.astype(BF16),
                             mem_w_v.astype(BF16), TOKEN_TILE)

    yp, ys = x_prompt, x_sample
    fk_p, fv_p, fl_p, cs_p = [], [], [], []
    fk_s, fv_s, fl_s, cs_s, gv_s = [], [], [], [], []
    for layer in range(depth):
        lg, lb = ln_g[layer], ln_b[layer]
        if layer % 2 == 0:
            e = layer // 2
            w_in = w_in_even[e].astype(BF16)
            qkv_w = w_in[:, 0:3 * FOX_DIM]
            f_w = jnp.pad(w_in[:, 3 * FOX_DIM:3 * FOX_DIM + FOX_HEADS], ((0, 0), (0, LANES - FOX_HEADS)))
            c_w = w_in[:, 3 * FOX_DIM + FOX_HEADS:]
            f_b = jnp.pad(b_forget[e], (0, LANES - FOX_HEADS)).reshape(1, LANES)
            w_out = w_out_even[e].astype(BF16)
            cw = conv_w[e]

            st0 = jnp.zeros((bp, SUBLANES, CONV_DIM), F32)
            q, kf, vf, kb, vb, lf, gc, cst = _even_inproj(yp, qkv_w, f_w, c_w, f_b, cw, st0, TOKEN_TILE)
            c = _cumsum_time(jnp.swapaxes(lf, 1, 2))
            att = _fox_prompt(q, kb, vb, c, FOX_TILE)
            yp = _even_outproj(yp.reshape(-1, d), att.reshape(-1, FOX_DIM), gc.reshape(-1, CONV_DIM),
                               w_out, row2(lg[0]), row2(lb[0]), alpha, TOKEN_TILE).reshape(bp, s_len, d)
            fk_p.append(kf.reshape(bp, s_len, FOX_HEADS, FOX_HEAD_DIM))
            fv_p.append(vf.reshape(bp, s_len, FOX_HEADS, FOX_HEAD_DIM))
            fl_p.append(lf)
            cs_p.append(cst)

            st = jnp.pad(state_conv[e], ((0, 0), (SUBLANES - (CONV_WIDTH - 1), 0), (0, 0)))
            q, kf, vf, kb, vb, lf, gc, cst = _even_inproj(ys, qkv_w, f_w, c_w, f_b, cw, st, t_new)
            n_keys = -(-(p_len + t_new) // LANES) * LANES
            pad = n_keys - p_len - t_new
            lt = jnp.concatenate([jnp.swapaxes(cache_fox_logf[e], 1, 2), jnp.swapaxes(lf, 1, 2),
                                  jnp.zeros((bs, FOX_HEADS, pad), F32)], axis=2)
            att = _fox_sample(q, kb, vb, cache_fox_k[e].reshape(bs, p_len, FOX_DIM),
                              cache_fox_v[e].reshape(bs, p_len, FOX_DIM), _cumsum_time(lt))
            ys = _even_outproj(ys.reshape(-1, d), att.reshape(-1, FOX_DIM), gc.reshape(-1, CONV_DIM),
                               w_out, row2(lg[0]), row2(lb[0]), alpha, bs * t_new).reshape(bs, t_new, d)
            fk_s.append(kf.reshape(bs, t_new, FOX_HEADS, FOX_HEAD_DIM))
            fv_s.append(vf.reshape(bs, t_new, FOX_HEADS, FOX_HEAD_DIM))
            fl_s.append(lf)
            cs_s.append(cst)
        else:
            o = layer // 2
            args = (w_in_odd[o].astype(BF16), row2(b_in_odd[o]), row2(gmlp_norm_g[o]), row2(gmlp_norm_b[o]),
                    gmlp_w_s[o], gmlp_b_s[o].T, w_out_odd[o].astype(BF16), row2(lg[0]), row2(lb[0]), alpha)
            yp, _ = _gmlp_mixer(yp.reshape(-1, d), *args, TOKEN_TILE, False)
            yp = yp.reshape(bp, s_len, d)
            ys_pad = jnp.pad(ys, ((0, 0), (0, GMLP_CHUNK - t_new), (0, 0))).reshape(-1, d)
            ys_pad, vn = _gmlp_mixer(ys_pad, *args, TOKEN_TILE, True)
            ys = ys_pad.reshape(bs, GMLP_CHUNK, d)[:, :t_new]
            gv_s.append(vn.reshape(bs, GMLP_CHUNK, GMLP_DIM)[:, :t_new])

        wq, wo = mem_w_q[layer].astype(BF16), mem_w_o[layer].astype(BF16)
        yp = _mem_attn(yp, mk_all[layer].reshape(bp, mem_len, MEM_DIM), mv_all[layer].reshape(bp, mem_len, MEM_DIM),
                       wq, wo, row2(lg[1]), row2(lb[1]), alpha, TOKEN_TILE)
        ys = _mem_attn(ys, cache_mem_k[layer].reshape(bs, mem_len, MEM_DIM),
                       cache_mem_v[layer].reshape(bs, mem_len, MEM_DIM),
                       wq, wo, row2(lg[1]), row2(lb[1]), alpha, t_new)

        wg = ffn_w_gate[layer].astype(BF16)
        wu = ffn_w_up[layer].astype(BF16)
        wd = ffn_w_down[layer].astype(BF16)
        yp = _ffn(yp.reshape(-1, d), wg, wu, wd, row2(lg[2]), row2(lb[2]), alpha, TOKEN_TILE).reshape(bp, s_len, d)
        ys = _ffn(ys.reshape(-1, d), wg, wu, wd, row2(lg[2]), row2(lb[2]), alpha, bs * t_new).reshape(bs, t_new, d)

    mshape = (depth, bp, mem_len, MEM_HEADS, MEM_HEAD_DIM)
    return (yp, ys, jnp.stack(fk_p), jnp.stack(fv_p), jnp.stack(fl_p), jnp.stack(cs_p),
            mk_all.reshape(mshape), mv_all.reshape(mshape),
            jnp.stack(fk_s), jnp.stack(fv_s), jnp.stack(fl_s), jnp.stack(cs_s), jnp.stack(gv_s))
```

```python
import functools

import jax
import jax.numpy as jnp
from jax import lax
from jax.experimental import pallas as pl
from jax.experimental.pallas import tpu as pltpu

F32 = jnp.float32
BF16 = jnp.bfloat16

D_MODEL = 1024
FOX_HEADS = 8
FOX_HEAD_DIM = 64
FOX_DIM = FOX_HEADS * FOX_HEAD_DIM
CONV_DIM = 512
CONV_WIDTH = 3
GMLP_CHUNK = 128
GMLP_GROUPS = 8
GMLP_DIM = D_MODEL
MEM_HEADS = 4
MEM_HEAD_DIM = 128
MEM_DIM = MEM_HEADS * MEM_HEAD_DIM
LN_EPS = 1e-5

LANES = 128
SUBLANES = 8
VMEM_LIMIT_BYTES = 56 * 1024 * 1024
TOKEN_TILE = 512
FOX_TILE = 512
FFN_CHUNK = 256
LOG2E = 1.4426950408889634


def _params(*semantics):
    return pltpu.CompilerParams(dimension_semantics=semantics,
                                vmem_limit_bytes=VMEM_LIMIT_BYTES)


def _resident(shape, layer=None):
    if layer is None:
        nd = len(shape)
        return pl.BlockSpec(shape, lambda *_: (0,) * nd, pipeline_mode=pl.Buffered(1))
    nd = len(shape) - 1
    return pl.BlockSpec((None,) + tuple(shape[1:]), lambda *_: (layer,) + (0,) * nd,
                        pipeline_mode=pl.Buffered(1))


def _dot(a, b):
    return jnp.dot(a, b, preferred_element_type=F32)


def _dot_nt(a, b):
    return lax.dot_general(a, b, (((1,), (1,)), ((), ())), preferred_element_type=F32)


def _layer_norm(y, g, b):
    mu = jnp.mean(y, axis=-1, keepdims=True)
    d = y - mu
    var = jnp.mean(d * d, axis=-1, keepdims=True)
    return d * lax.rsqrt(var + LN_EPS) * g + b


def _gelu(z):
    return 0.5 * z * (1.0 + lax.erf(z * (2.0 ** -0.5)))


def _lane_prefix_sum(x, carry):
    rows, n = x.shape
    lane = lax.broadcasted_iota(jnp.int32, (rows, LANES), 1)
    segs = []
    for j in range(n // LANES):
        seg = x[:, j * LANES:(j + 1) * LANES]
        step = 1
        while step < LANES:
            seg = seg + jnp.where(lane >= step, pltpu.roll(seg, step, 1), 0.0)
            step *= 2
        segs.append(seg)
    outs = []
    for seg in segs:
        seg = seg + carry
        outs.append(seg)
        carry = seg[:, LANES - 1:LANES]
    return jnp.concatenate(outs, axis=1), carry


def _even_inproj_kernel(x_ref, wqkv_ref, wft_ref, wc_ref, bf_ref, cw_ref, st_ref,
                        q_ref, kf_ref, vf_ref, kb_ref, vb_ref, lf_ref, gc_ref, cs_ref, *rest,
                        tm, nt, emit_c):
    if emit_c:
        c_ref, prev_ref, ccarry_ref = rest
    else:
        (prev_ref,) = rest
    t = pl.program_id(1)

    @pl.when(t == 0)
    def _():
        prev_ref[...] = st_ref[0]
        if emit_c:
            ccarry_ref[...] = jnp.zeros_like(ccarry_ref)

    xb = x_ref[0].astype(BF16)
    q = _dot(xb, wqkv_ref[:, 0:FOX_DIM])
    q_ref[0] = (q * (LOG2E * FOX_HEAD_DIM ** -0.5)).astype(BF16)
    k = _dot(xb, wqkv_ref[:, FOX_DIM:2 * FOX_DIM])
    kf_ref[0] = k
    kb_ref[0] = k.astype(BF16)
    v = _dot(xb, wqkv_ref[:, 2 * FOX_DIM:3 * FOX_DIM])
    vf_ref[0] = v
    vb_ref[0] = v.astype(BF16)

    lf = jax.nn.log_sigmoid(_dot_nt(wft_ref[...], xb) + bf_ref[...])
    lf_ref[0] = lf
    if emit_c:
        c, carry = _lane_prefix_sum(lf, ccarry_ref[...])
        c_ref[0] = c
        ccarry_ref[...] = carry

    h = _dot(xb, wc_ref[:, 0:CONV_DIM])
    bg = _dot(xb, wc_ref[:, CONV_DIM:2 * CONV_DIM])
    cg = _dot(xb, wc_ref[:, 2 * CONV_DIM:3 * CONV_DIM])
    pre = cg * h

    prev = prev_ref[...]
    row = lax.broadcasted_iota(jnp.int32, (tm, CONV_DIM), 0)
    p1 = jnp.where(row == 0, prev[7:8, :], pltpu.roll(pre, 1, 0))
    p2 = jnp.where(row == 0, prev[6:7, :],
                   jnp.where(row == 1, prev[7:8, :], pltpu.roll(pre, 2, 0)))
    cw = cw_ref[...]
    conv = cw[0:1, :] * p2 + cw[1:2, :] * p1 + cw[2:3, :] * pre
    gc_ref[0] = (bg * conv).astype(BF16)
    prev_ref[...] = pre[tm - SUBLANES:tm, :]

    @pl.when(t == nt - 1)
    def _():
        cs_ref[0] = pre[tm - (CONV_WIDTH - 1):tm, :]


def _even_inproj(x, wqkv, wft, wc, bf, cw, state8, tm, emit_c):
    bn, tlen, _ = x.shape
    nt = tlen // tm
    tok = lambda w: pl.BlockSpec((1, tm, w), lambda b, t: (b, t, 0))
    tlane = pl.BlockSpec((1, FOX_HEADS, tm), lambda b, t: (b, 0, t))
    out_shape = [
        jax.ShapeDtypeStruct((bn, tlen, FOX_DIM), BF16),
        jax.ShapeDtypeStruct((bn, tlen, FOX_DIM), F32),
        jax.ShapeDtypeStruct((bn, tlen, FOX_DIM), F32),
        jax.ShapeDtypeStruct((bn, tlen, FOX_DIM), BF16),
        jax.ShapeDtypeStruct((bn, tlen, FOX_DIM), BF16),
        jax.ShapeDtypeStruct((bn, FOX_HEADS, tlen), F32),
        jax.ShapeDtypeStruct((bn, tlen, CONV_DIM), BF16),
        jax.ShapeDtypeStruct((bn, CONV_WIDTH - 1, CONV_DIM), F32),
    ]
    out_specs = [tok(FOX_DIM), tok(FOX_DIM), tok(FOX_DIM), tok(FOX_DIM), tok(FOX_DIM), tlane,
                 tok(CONV_DIM), pl.BlockSpec((1, CONV_WIDTH - 1, CONV_DIM), lambda b, t: (b, 0, 0))]
    scratch = [pltpu.VMEM((SUBLANES, CONV_DIM), F32)]
    if emit_c:
        out_shape.append(jax.ShapeDtypeStruct((bn, FOX_HEADS, tlen), F32))
        out_specs.append(tlane)
        scratch.append(pltpu.VMEM((FOX_HEADS, 1), F32))
    return pl.pallas_call(
        functools.partial(_even_inproj_kernel, tm=tm, nt=nt, emit_c=emit_c),
        grid=(bn, nt),
        in_specs=[tok(D_MODEL), _resident(wqkv.shape), _resident(wft.shape),
                  _resident(wc.shape), _resident(bf.shape), _resident(cw.shape),
                  pl.BlockSpec((1, SUBLANES, CONV_DIM), lambda b, t: (b, 0, 0))],
        out_specs=tuple(out_specs),
        out_shape=tuple(out_shape),
        scratch_shapes=scratch,
        compiler_params=_params("parallel", "arbitrary"),
        name="even_inproj",
    )(x, wqkv, wft, wc, bf, cw, state8)


def _head_mask(shape, hh):
    lane = lax.broadcasted_iota(jnp.int32, shape, 1)
    return (lane >= FOX_HEAD_DIM) if hh == 1 else (lane < FOX_HEAD_DIM)


def _fox_prompt_kernel(q_ref, k_ref, v_ref, c_ref, o_ref, vaug_ref, *, tq, nq):
    hp = pl.program_id(1)
    i = pl.program_id(2)

    @pl.when(i == 0)
    def _():
        v = v_ref[0]
        for hh in range(2):
            vaug_ref[hh] = jnp.where(_head_mask(v.shape, hh), v, jnp.ones_like(v))

    q = q_ref[0]
    qh = [jnp.where(_head_mask(q.shape, hh), q, jnp.zeros_like(q)) for hh in range(2)]
    c_q0 = [c_ref[0, (hp * 2 + hh) * nq + i][:, 0:1] for hh in range(2)]
    row = lax.broadcasted_iota(jnp.int32, (tq, tq), 0)
    col = lax.broadcasted_iota(jnp.int32, (tq, tq), 1)

    def scores(j, hh):
        kj = k_ref[0, pl.ds(pl.multiple_of(j * tq, tq), tq), :]
        bias = (c_q0[hh] - c_ref[0, (hp * 2 + hh) * nq + j]) * LOG2E
        return _dot_nt(qh[hh], kj) + bias

    def pv(p, j, hh):
        return _dot(p, vaug_ref[hh, pl.ds(pl.multiple_of(j * tq, tq), tq), :])

    state = []
    for hh in range(2):
        s = jnp.where(col <= row, scores(i, hh), -jnp.inf)
        m = jnp.max(s, axis=-1, keepdims=True)
        state.append((m, pv(jnp.exp2(s - m).astype(BF16), i, hh)))

    def body(j, carry):
        ss = [scores(j, hh) for hh in range(2)]
        new = []
        for hh in range(2):
            m, acc = carry[hh]
            m_new = jnp.maximum(m, jnp.max(ss[hh], axis=-1, keepdims=True))
            p = jnp.exp2(ss[hh] - m_new).astype(BF16)
            new.append((m_new, jnp.exp2(m - m_new) * acc + pv(p, j, hh)))
        return tuple(new)

    carry = lax.fori_loop(0, i, body, tuple(state))
    outs = [acc / pltpu.roll(acc, FOX_HEAD_DIM, 1) for _, acc in carry]
    o_ref[0] = jnp.where(_head_mask(outs[0].shape, 0), outs[0], outs[1]).astype(o_ref.dtype)


def _fox_prompt(q, k, v, c, tq):
    bn, s_len, _ = q.shape
    nq = s_len // tq
    c4 = c.reshape(bn, FOX_HEADS * nq, 1, tq)
    qspec = pl.BlockSpec((1, tq, LANES), lambda b, h, i: (b, i, h))
    kspec = pl.BlockSpec((1, s_len, LANES), lambda b, h, i: (b, 0, h))
    return pl.pallas_call(
        functools.partial(_fox_prompt_kernel, tq=tq, nq=nq),
        grid=(bn, FOX_DIM // LANES, nq),
        in_specs=[qspec, kspec, kspec,
                  pl.BlockSpec((1, FOX_HEADS * nq, 1, tq), lambda b, h, i: (b, 0, 0, 0))],
        out_specs=qspec,
        out_shape=jax.ShapeDtypeStruct((bn, s_len, FOX_DIM), BF16),
        scratch_shapes=[pltpu.VMEM((2, s_len, LANES), BF16)],
        compiler_params=_params("parallel", "parallel", "arbitrary"),
        name="fox_prompt",
    )(q, k, v, c4)


def _fox_sample_kernel(q_ref, kn_ref, vn_ref, kc_ref, vc_ref, lf_ref, o_ref, *, t_new, p_len):
    hp = pl.program_id(1)
    q = q_ref[0]
    n_new = lf_ref.shape[-1] - p_len
    zeros = jnp.zeros((n_new - t_new, LANES), BF16)
    kn = jnp.concatenate([kn_ref[0], zeros], axis=0)
    vn = jnp.concatenate([vn_ref[0], zeros], axis=0)
    kc = kc_ref[0].astype(BF16)
    vc = vc_ref[0].astype(BF16)
    row = lax.broadcasted_iota(jnp.int32, (t_new, n_new), 0)
    col = lax.broadcasted_iota(jnp.int32, (t_new, n_new), 1)
    c_all, _ = _lane_prefix_sum(lf_ref[0], jnp.zeros((FOX_HEADS, 1), F32))
    head_row = lax.broadcasted_iota(jnp.int32, c_all.shape, 0)
    outs = []
    for hh in range(2):
        head = hp * 2 + hh
        qh = jnp.where(_head_mask(q.shape, hh), q, jnp.zeros_like(q))
        c_row = jnp.sum(jnp.where(head_row == head, c_all, 0.0), axis=0, keepdims=True)
        bias = (c_row[:, p_len - 1:p_len] - c_row) * LOG2E
        s_c = _dot_nt(qh, kc) + bias[:, 0:p_len]
        s_n = jnp.where(col <= row, _dot_nt(qh, kn) + bias[:, p_len:], -jnp.inf)
        m = jnp.maximum(jnp.max(s_c, axis=-1, keepdims=True), jnp.max(s_n, axis=-1, keepdims=True))
        p_c = jnp.exp2(s_c - m)
        p_n = jnp.exp2(s_n - m)
        l = jnp.sum(p_c, axis=-1, keepdims=True) + jnp.sum(p_n, axis=-1, keepdims=True)
        outs.append((_dot(p_c.astype(BF16), vc) + _dot(p_n.astype(BF16), vn)) / l)
    o_ref[0] = jnp.where(_head_mask(outs[0].shape, 0), outs[0], outs[1]).astype(o_ref.dtype)


def _fox_sample(q, kn, vn, kc, vc, lf_all):
    bn, t_new, _ = q.shape
    p_len = kc.shape[1]
    n_keys = lf_all.shape[-1]
    qspec = pl.BlockSpec((1, t_new, LANES), lambda b, h: (b, 0, h))
    cspec = pl.BlockSpec((1, p_len, LANES), lambda b, h: (b, 0, h))
    return pl.pallas_call(
        functools.partial(_fox_sample_kernel, t_new=t_new, p_len=p_len),
        grid=(bn, FOX_DIM // LANES),
        in_specs=[qspec, qspec, qspec, cspec, cspec,
                  pl.BlockSpec((1, FOX_HEADS, n_keys), lambda b, h: (b, 0, 0))],
        out_specs=qspec,
        out_shape=jax.ShapeDtypeStruct((bn, t_new, FOX_DIM), BF16),
        compiler_params=_params("parallel", "parallel"),
        name="fox_sample",
    )(q, kn, vn, kc, vc, lf_all)


def _gmlp_kernel(x_ref, win_ref, bin_ref, ng_ref, nb_ref, ws_ref, bst_ref, wout_ref,
                 g_ref, b_ref, y_ref, *rest, tm, alpha, emit_vn):
    if emit_vn:
        vn_ref, u_scr, vnb_scr, gated_scr = rest
    else:
        u_scr, vnb_scr, gated_scr = rest
    x = x_ref[...]
    xb = x.astype(BF16)
    u_scr[...] = _gelu(_dot(xb, win_ref[:, 0:GMLP_DIM]) + bin_ref[:, 0:GMLP_DIM])
    v = _gelu(_dot(xb, win_ref[:, GMLP_DIM:]) + bin_ref[:, GMLP_DIM:])
    vn = _layer_norm(v, ng_ref[...], nb_ref[...])
    if emit_vn:
        vn_ref[...] = vn
    vnb_scr[...] = vn.astype(BF16)

    row = lax.broadcasted_iota(jnp.int32, (GMLP_CHUNK, GMLP_CHUNK), 0)
    col = lax.broadcasted_iota(jnp.int32, (GMLP_CHUNK, GMLP_CHUNK), 1)
    gd = GMLP_DIM // GMLP_GROUPS
    n_chunks = tm // GMLP_CHUNK
    for g in range(GMLP_GROUPS):
        wm = jnp.where(col <= row, ws_ref[g], 0.0).astype(BF16)
        bias = bst_ref[:, g:g + 1]
        cs = slice(g * gd, (g + 1) * gd)
        for r in range(0, n_chunks, 2):
            rss = [slice(rr * GMLP_CHUNK, (rr + 1) * GMLP_CHUNK) for rr in range(r, min(r + 2, n_chunks))]
            mixed = _dot(wm, jnp.concatenate([vnb_scr[rs, cs] for rs in rss], axis=1)) + bias
            for n, rs in enumerate(rss):
                gated_scr[rs, cs] = (u_scr[rs, cs] * mixed[:, n * gd:(n + 1) * gd]).astype(BF16)
    out = _dot(gated_scr[...], wout_ref[...])
    y_ref[...] = _layer_norm(alpha * x + out, g_ref[...], b_ref[...])


def _gmlp_mixer(x, layer, win, bin_, ng, nb, ws, bst, wout, g, b, alpha, tm, emit_vn):
    n = x.shape[0]
    tok = pl.BlockSpec((tm, D_MODEL), lambda i: (i, 0))
    out_shape = [jax.ShapeDtypeStruct((n, D_MODEL), F32)]
    out_specs = [tok]
    if emit_vn:
        out_shape.append(jax.ShapeDtypeStruct((n, GMLP_DIM), F32))
        out_specs.append(tok)
    res = pl.pallas_call(
        functools.partial(_gmlp_kernel, tm=tm, alpha=alpha, emit_vn=emit_vn),
        grid=(n // tm,),
        in_specs=[tok, _resident(win.shape, layer), _resident(bin_.shape), _resident(ng.shape),
                  _resident(nb.shape), _resident(ws.shape, layer), _resident(bst.shape),
                  _resident(wout.shape, layer), _resident(g.shape), _resident(b.shape)],
        out_specs=tuple(out_specs),
        out_shape=tuple(out_shape),
        scratch_shapes=[pltpu.VMEM((tm, GMLP_DIM), F32), pltpu.VMEM((tm, GMLP_DIM), BF16),
                        pltpu.VMEM((tm, GMLP_DIM), BF16)],
        compiler_params=_params("parallel"),
        name="gmlp_mixer",
    )(x, win, bin_, ng, nb, ws, bst, wout, g, b)
    return res if emit_vn else (res[0], None)


def _mem_kv_kernel(m_ref, wk_ref, wv_ref, k_ref, v_ref):
    mb = m_ref[0].astype(BF16)
    k = _dot(mb, wk_ref[0])
    v = _dot(mb, wv_ref[0])
    for h in range(MEM_HEADS):
        hs = slice(h * MEM_HEAD_DIM, (h + 1) * MEM_HEAD_DIM)
        k_ref[0, 0, :, h, :] = k[:, hs]
        v_ref[0, 0, :, h, :] = v[:, hs]


def _mem_kv(mem, wk, wv):
    bn, mlen, _ = mem.shape
    depth = wk.shape[0]
    wspec = pl.BlockSpec((1, D_MODEL, MEM_DIM), lambda l, b: (l, 0, 0))
    ospec = pl.BlockSpec((1, 1, mlen, MEM_HEADS, MEM_HEAD_DIM), lambda l, b: (l, b, 0, 0, 0))
    shape = jax.ShapeDtypeStruct((depth, bn, mlen, MEM_HEADS, MEM_HEAD_DIM), F32)
    return pl.pallas_call(
        _mem_kv_kernel,
        grid=(depth, bn),
        in_specs=[pl.BlockSpec((1, mlen, D_MODEL), lambda l, b: (b, 0, 0)), wspec, wspec],
        out_specs=(ospec, ospec), out_shape=(shape, shape),
        compiler_params=_params("parallel", "parallel"),
        name="mem_kv",
    )(mem, wk, wv)


def _mem_attn_kernel(x_ref, *refs, alpha, fuse_even):
    if fuse_even:
        att_ref, gc_ref, wmix_ref, g0_ref, b0_ref = refs[:5]
        refs = refs[5:]
    mk_ref, mv_ref, wq_ref, wo_ref, g_ref, b_ref, y_ref = refs
    x = x_ref[0]
    if fuse_even:
        mix = _dot(att_ref[0], wmix_ref[0:FOX_DIM, :]) + _dot(gc_ref[0], wmix_ref[FOX_DIM:, :])
        x = _layer_norm(alpha * x + mix, g0_ref[...], b0_ref[...])
    q = _dot(x.astype(BF16), wq_ref[...]) * (MEM_HEAD_DIM ** -0.5)
    qb = q.astype(BF16)
    heads = []
    for h in range(MEM_HEADS):
        hs = slice(h * MEM_HEAD_DIM, (h + 1) * MEM_HEAD_DIM)
        mk = mk_ref[0, 0, :, h, :].astype(BF16)
        mv = mv_ref[0, 0, :, h, :].astype(BF16)
        s = _dot_nt(qb[:, hs], mk)
        m = jnp.max(s, axis=-1, keepdims=True)
        p = jnp.exp(s - m)
        l = jnp.sum(p, axis=-1, keepdims=True)
        heads.append((_dot(p.astype(BF16), mv) / l).astype(BF16))
    o = jnp.concatenate(heads, axis=-1)
    y_ref[0] = _layer_norm(alpha * x + _dot(o, wo_ref[...]), g_ref[...], b_ref[...])


def _mem_attn(x, mk, mv, layer, wq, wo, g, b, alpha, tm, even=None):
    bn, tlen, _ = x.shape
    mlen = mk.shape[2]
    tok = lambda w: pl.BlockSpec((1, tm, w), lambda bb, t: (bb, t, 0))
    mspec = pl.BlockSpec((1, 1, mlen, MEM_HEADS, MEM_HEAD_DIM), lambda bb, t: (layer, bb, 0, 0, 0))
    pre_args, pre_specs = [], []
    if even is not None:
        att, gc, wmix, e, g0, b0 = even
        pre_args = [att, gc, wmix, g0, b0]
        pre_specs = [tok(FOX_DIM), tok(CONV_DIM), _resident(wmix.shape, e), _resident(g0.shape),
                     _resident(b0.shape)]
    return pl.pallas_call(
        functools.partial(_mem_attn_kernel, alpha=alpha, fuse_even=even is not None),
        grid=(bn, tlen // tm),
        in_specs=[tok(D_MODEL)] + pre_specs + [mspec, mspec, _resident(wq.shape, layer),
                                               _resident(wo.shape, layer), _resident(g.shape),
                                               _resident(b.shape)],
        out_specs=tok(D_MODEL),
        out_shape=jax.ShapeDtypeStruct((bn, tlen, D_MODEL), F32),
        compiler_params=_params("parallel", "parallel"),
        name="mem_attn",
    )(x, *pre_args, mk, mv, wq, wo, g, b)


def _ffn_kernel(x_ref, wg_ref, wu_ref, wd_ref, g_ref, b_ref, y_ref, h_scr, *, alpha):
    x = x_ref[...]
    xb = x.astype(BF16)
    d_ff = wg_ref.shape[1]
    for c in range(d_ff // FFN_CHUNK):
        cols = slice(c * FFN_CHUNK, (c + 1) * FFN_CHUNK)
        gate = _dot(xb, wg_ref[:, cols])
        up = _dot(xb, wu_ref[:, cols])
        h_scr[:, cols] = (gate * jax.nn.sigmoid(gate) * up).astype(BF16)
    y = alpha * x + _dot(h_scr[...], wd_ref[...])
    y_ref[...] = _layer_norm(y, g_ref[...], b_ref[...])


def _ffn(x, layer, wg, wu, wd, g, b, alpha, tm):
    n = x.shape[0]
    d_ff = wg.shape[-1]
    assert d_ff % FFN_CHUNK == 0
    tok = pl.BlockSpec((tm, D_MODEL), lambda i: (i, 0))
    return pl.pallas_call(
        functools.partial(_ffn_kernel, alpha=alpha),
        grid=(n // tm,),
        in_specs=[tok, _resident(wg.shape, layer), _resident(wu.shape, layer), _resident(wd.shape, layer),
                  _resident(g.shape), _resident(b.shape)],
        out_specs=tok,
        out_shape=jax.ShapeDtypeStruct((n, D_MODEL), F32),
        scratch_shapes=[pltpu.VMEM((tm, d_ff), BF16)],
        compiler_params=_params("parallel"),
        name="swiglu_ffn",
    )(x, wg, wu, wd, g, b)


def kernel(x_prompt, x_sample, cache_fox_k, cache_fox_v, cache_fox_logf, state_conv, cache_mem_k, cache_mem_v, mem_prompt, w_in_even, b_forget, conv_w, w_out_even, w_in_odd, b_in_odd, gmlp_norm_g, gmlp_norm_b, gmlp_w_s, gmlp_b_s, w_out_odd, mem_w_q, mem_w_k, mem_w_v, mem_w_o, ffn_w_gate, ffn_w_up, ffn_w_down, ln_g, ln_b):
    depth = ln_g.shape[0]
    alpha = float((2 * depth) ** 0.25)
    bp, s_len, d = x_prompt.shape
    bs, t_new, _ = x_sample.shape
    p_len = cache_fox_k.shape[2]
    row2 = lambda a: a.reshape(1, -1)

    w_out_even_b, w_in_odd_b, w_out_odd_b = (w.astype(BF16) for w in (w_out_even, w_in_odd, w_out_odd))
    mem_wq_b, mem_wo_b = mem_w_q.astype(BF16), mem_w_o.astype(BF16)
    ffn_wg_b, ffn_wu_b, ffn_wd_b = (w.astype(BF16) for w in (ffn_w_gate, ffn_w_up, ffn_w_down))

    mk_p, mv_p = _mem_kv(mem_prompt, mem_w_k.astype(BF16), mem_w_v.astype(BF16))

    yp, ys = x_prompt, x_sample
    fk_p, fv_p, fl_p, cs_p = [], [], [], []
    fk_s, fv_s, fl_s, cs_s, gv_s = [], [], [], [], []
    for layer in range(depth):
        lg, lb = ln_g[layer], ln_b[layer]
        even_p = even_s = None
        if layer % 2 == 0:
            e = layer // 2
            w_in = w_in_even[e].astype(BF16)
            qkv_w = w_in[:, 0:3 * FOX_DIM]
            f_wt = w_in[:, 3 * FOX_DIM:3 * FOX_DIM + FOX_HEADS].T
            c_w = w_in[:, 3 * FOX_DIM + FOX_HEADS:]
            f_b = b_forget[e].reshape(FOX_HEADS, 1)
            cw = conv_w[e]

            st0 = jnp.zeros((bp, SUBLANES, CONV_DIM), F32)
            q, kf, vf, kb, vb, lf, gc, cst, c = _even_inproj(yp, qkv_w, f_wt, c_w, f_b, cw, st0,
                                                             TOKEN_TILE, True)
            att = _fox_prompt(q, kb, vb, c, FOX_TILE)
            even_p = (att, gc, w_out_even_b, e, row2(lg[0]), row2(lb[0]))
            fk_p.append(kf.reshape(bp, s_len, FOX_HEADS, FOX_HEAD_DIM))
            fv_p.append(vf.reshape(bp, s_len, FOX_HEADS, FOX_HEAD_DIM))
            fl_p.append(jnp.swapaxes(lf, 1, 2))
            cs_p.append(cst)

            st = jnp.pad(state_conv[e], ((0, 0), (SUBLANES - (CONV_WIDTH - 1), 0), (0, 0)))
            q, kf, vf, kb, vb, lf, gc, cst = _even_inproj(ys, qkv_w, f_wt, c_w, f_b, cw, st, t_new, False)
            pad = -(p_len + t_new) % LANES
            lf_all = jnp.concatenate([jnp.swapaxes(cache_fox_logf[e], 1, 2), lf,
                                      jnp.zeros((bs, FOX_HEADS, pad), F32)], axis=2)
            att = _fox_sample(q, kb, vb, cache_fox_k[e].reshape(bs, p_len, FOX_DIM),
                              cache_fox_v[e].reshape(bs, p_len, FOX_DIM), lf_all)
            even_s = (att, gc, w_out_even_b, e, row2(lg[0]), row2(lb[0]))
            fk_s.append(kf.reshape(bs, t_new, FOX_HEADS, FOX_HEAD_DIM))
            fv_s.append(vf.reshape(bs, t_new, FOX_HEADS, FOX_HEAD_DIM))
            fl_s.append(jnp.swapaxes(lf, 1, 2))
            cs_s.append(cst)
        else:
            o = layer // 2
            args = (o, w_in_odd_b, row2(b_in_odd[o]), row2(gmlp_norm_g[o]), row2(gmlp_norm_b[o]),
                    gmlp_w_s, gmlp_b_s[o].T, w_out_odd_b, row2(lg[0]), row2(lb[0]), alpha)
            yp, _ = _gmlp_mixer(yp.reshape(-1, d), *args, TOKEN_TILE, False)
            yp = yp.reshape(bp, s_len, d)
            ys_pad = jnp.pad(ys, ((0, 0), (0, GMLP_CHUNK - t_new), (0, 0))).reshape(-1, d)
            ys_pad, vn = _gmlp_mixer(ys_pad, *args, TOKEN_TILE, True)
            ys = ys_pad.reshape(bs, GMLP_CHUNK, d)[:, :t_new]
            gv_s.append(vn.reshape(bs, GMLP_CHUNK, GMLP_DIM)[:, :t_new])

        yp = _mem_attn(yp, mk_p, mv_p, layer, mem_wq_b, mem_wo_b, row2(lg[1]), row2(lb[1]), alpha,
                       TOKEN_TILE, even_p)
        ys = _mem_attn(ys, cache_mem_k, cache_mem_v, layer, mem_wq_b, mem_wo_b, row2(lg[1]), row2(lb[1]),
                       alpha, t_new, even_s)

        ffn_args = (layer, ffn_wg_b, ffn_wu_b, ffn_wd_b, row2(lg[2]), row2(lb[2]), alpha)
        yp = _ffn(yp.reshape(-1, d), *ffn_args, TOKEN_TILE).reshape(bp, s_len, d)
        ys = _ffn(ys.reshape(-1, d), *ffn_args, bs * t_new).reshape(bs, t_new, d)

    return (yp, ys, jnp.stack(fk_p), jnp.stack(fv_p), jnp.stack(fl_p), jnp.stack(cs_p), mk_p, mv_p,
            jnp.stack(fk_s), jnp.stack(fv_s), jnp.stack(fl_s), jnp.stack(cs_s), jnp.stack(gv_s))
```

```python
import functools

import jax
import jax.numpy as jnp
from jax import lax
from jax.experimental import pallas as pl
from jax.experimental.pallas import tpu as pltpu

F32 = jnp.float32
BF16 = jnp.bfloat16

D_MODEL = 1024
FOX_HEADS = 8
FOX_HEAD_DIM = 64
FOX_DIM = FOX_HEADS * FOX_HEAD_DIM
CONV_DIM = 512
CONV_WIDTH = 3
GMLP_CHUNK = 128
GMLP_GROUPS = 8
GMLP_DIM = D_MODEL
MEM_HEADS = 4
MEM_HEAD_DIM = 128
MEM_DIM = MEM_HEADS * MEM_HEAD_DIM
LN_EPS = 1e-5

LANES = 128
SUBLANES = 8
VMEM_LIMIT_BYTES = 56 * 1024 * 1024
TOKEN_TILE = 512
FOX_TILE = 512
FOX_QUERY_SPLIT = 1
FOX_KEY_SPLIT = 2
FOX_TILES_PER_STEP = 2
FFN_CHUNK = 256
LOG2E = 1.4426950408889634


def _params(*semantics):
    return pltpu.CompilerParams(dimension_semantics=semantics,
                                vmem_limit_bytes=VMEM_LIMIT_BYTES)


def _resident(shape, layer=None):
    if layer is None:
        nd = len(shape)
        return pl.BlockSpec(shape, lambda *_: (0,) * nd, pipeline_mode=pl.Buffered(1))
    nd = len(shape) - 1
    return pl.BlockSpec((None,) + tuple(shape[1:]), lambda *_: (layer,) + (0,) * nd,
                        pipeline_mode=pl.Buffered(1))


def _dot(a, b):
    return jnp.dot(a, b, preferred_element_type=F32)


def _dot_nt(a, b):
    return lax.dot_general(a, b, (((1,), (1,)), ((), ())), preferred_element_type=F32)


def _layer_norm(y, g, b):
    mu = jnp.mean(y, axis=-1, keepdims=True)
    d = y - mu
    var = jnp.mean(d * d, axis=-1, keepdims=True)
    return d * lax.rsqrt(var + LN_EPS) * g + b


def _gelu(z):
    return 0.5 * z * (1.0 + lax.erf(z * (2.0 ** -0.5)))


def _lane_prefix_sum(x, carry):
    rows, n = x.shape
    lane = lax.broadcasted_iota(jnp.int32, (rows, LANES), 1)
    segs = []
    for j in range(n // LANES):
        seg = x[:, j * LANES:(j + 1) * LANES]
        step = 1
        while step < LANES:
            seg = seg + jnp.where(lane >= step, pltpu.roll(seg, step, 1), 0.0)
            step *= 2
        segs.append(seg)
    outs = []
    for seg in segs:
        seg = seg + carry
        outs.append(seg)
        carry = seg[:, LANES - 1:LANES]
    return jnp.concatenate(outs, axis=1), carry


def _even_inproj_kernel(x_ref, wqkv_ref, wft_ref, wc_ref, bf_ref, cw_ref, st_ref,
                        q_ref, kf_ref, vf_ref, kb_ref, vb_ref, lf_ref, gc_ref, cs_ref, *rest,
                        tm, nt, emit_c):
    if emit_c:
        c_ref, prev_ref, ccarry_ref = rest
    else:
        (prev_ref,) = rest
    t = pl.program_id(1)

    @pl.when(t == 0)
    def _():
        prev_ref[...] = st_ref[0]
        if emit_c:
            ccarry_ref[...] = jnp.zeros_like(ccarry_ref)

    xb = x_ref[0].astype(BF16)
    q = _dot(xb, wqkv_ref[:, 0:FOX_DIM])
    q_ref[0] = (q * (LOG2E * FOX_HEAD_DIM ** -0.5)).astype(BF16)
    k = _dot(xb, wqkv_ref[:, FOX_DIM:2 * FOX_DIM])
    kf_ref[0] = k
    kb_ref[0] = k.astype(BF16)
    v = _dot(xb, wqkv_ref[:, 2 * FOX_DIM:3 * FOX_DIM])
    vf_ref[0] = v
    vb_ref[0] = v.astype(BF16)

    lf = jax.nn.log_sigmoid(_dot_nt(wft_ref[...], xb) + bf_ref[...])
    lf_ref[0] = lf
    if emit_c:
        c, carry = _lane_prefix_sum(lf, ccarry_ref[...])
        c_ref[0] = c
        ccarry_ref[...] = carry

    h = _dot(xb, wc_ref[:, 0:CONV_DIM])
    bg = _dot(xb, wc_ref[:, CONV_DIM:2 * CONV_DIM])
    cg = _dot(xb, wc_ref[:, 2 * CONV_DIM:3 * CONV_DIM])
    pre = cg * h

    prev = prev_ref[...]
    row = lax.broadcasted_iota(jnp.int32, (tm, CONV_DIM), 0)
    p1 = jnp.where(row == 0, prev[7:8, :], pltpu.roll(pre, 1, 0))
    p2 = jnp.where(row == 0, prev[6:7, :],
                   jnp.where(row == 1, prev[7:8, :], pltpu.roll(pre, 2, 0)))
    cw = cw_ref[...]
    conv = cw[0:1, :] * p2 + cw[1:2, :] * p1 + cw[2:3, :] * pre
    gc_ref[0] = (bg * conv).astype(BF16)
    prev_ref[...] = pre[tm - SUBLANES:tm, :]

    @pl.when(t == nt - 1)
    def _():
        cs_ref[0] = pre[tm - (CONV_WIDTH - 1):tm, :]


def _even_inproj(x, wqkv, wft, wc, bf, cw, state8, tm, emit_c):
    bn, tlen, _ = x.shape
    nt = tlen // tm
    tok = lambda w: pl.BlockSpec((1, tm, w), lambda b, t: (b, t, 0))
    tlane = pl.BlockSpec((1, FOX_HEADS, tm), lambda b, t: (b, 0, t))
    out_shape = [
        jax.ShapeDtypeStruct((bn, tlen, FOX_DIM), BF16),
        jax.ShapeDtypeStruct((bn, tlen, FOX_DIM), F32),
        jax.ShapeDtypeStruct((bn, tlen, FOX_DIM), F32),
        jax.ShapeDtypeStruct((bn, tlen, FOX_DIM), BF16),
        jax.ShapeDtypeStruct((bn, tlen, FOX_DIM), BF16),
        jax.ShapeDtypeStruct((bn, FOX_HEADS, tlen), F32),
        jax.ShapeDtypeStruct((bn, tlen, CONV_DIM), BF16),
        jax.ShapeDtypeStruct((bn, CONV_WIDTH - 1, CONV_DIM), F32),
    ]
    out_specs = [tok(FOX_DIM), tok(FOX_DIM), tok(FOX_DIM), tok(FOX_DIM), tok(FOX_DIM), tlane,
                 tok(CONV_DIM), pl.BlockSpec((1, CONV_WIDTH - 1, CONV_DIM), lambda b, t: (b, 0, 0))]
    scratch = [pltpu.VMEM((SUBLANES, CONV_DIM), F32)]
    if emit_c:
        out_shape.append(jax.ShapeDtypeStruct((bn, FOX_HEADS, tlen), F32))
        out_specs.append(tlane)
        scratch.append(pltpu.VMEM((FOX_HEADS, 1), F32))
    return pl.pallas_call(
        functools.partial(_even_inproj_kernel, tm=tm, nt=nt, emit_c=emit_c),
        grid=(bn, nt),
        in_specs=[tok(D_MODEL), _resident(wqkv.shape), _resident(wft.shape),
                  _resident(wc.shape), _resident(bf.shape), _resident(cw.shape),
                  pl.BlockSpec((1, SUBLANES, CONV_DIM), lambda b, t: (b, 0, 0))],
        out_specs=tuple(out_specs),
        out_shape=tuple(out_shape),
        scratch_shapes=scratch,
        compiler_params=_params("parallel", "arbitrary"),
        name="even_inproj",
    )(x, wqkv, wft, wc, bf, cw, state8)


def _head_mask(shape, hh):
    lane = lax.broadcasted_iota(jnp.int32, shape, 1)
    return (lane >= FOX_HEAD_DIM) if hh == 1 else (lane < FOX_HEAD_DIM)


def _fox_prompt_kernel(q_ref, k_ref, v_ref, c_ref, o_ref, vaug_ref, *, tq, nq):
    hp = pl.program_id(1)
    i = pl.program_id(2)

    @pl.when(i == 0)
    def _():
        v = v_ref[0]
        for hh in range(2):
            vaug_ref[hh] = jnp.where(_head_mask(v.shape, hh), v, jnp.ones_like(v))

    q = q_ref[0]
    qh = [jnp.where(_head_mask(q.shape, hh), q, jnp.zeros_like(q)) for hh in range(2)]
    c_q0 = [c_ref[0, (hp * 2 + hh) * nq + i][:, 0:1] for hh in range(2)]
    row = lax.broadcasted_iota(jnp.int32, (tq, tq), 0)
    col = lax.broadcasted_iota(jnp.int32, (tq, tq), 1)

    def scores(j, hh):
        kj = k_ref[0, pl.ds(pl.multiple_of(j * tq, tq), tq), :]
        bias = (c_q0[hh] - c_ref[0, (hp * 2 + hh) * nq + j]) * LOG2E
        return _dot_nt(qh[hh], kj) + bias

    def pv(p, j, hh):
        return _dot(p, vaug_ref[hh, pl.ds(pl.multiple_of(j * tq, tq), tq), :])

    state = []
    for hh in range(2):
        s = jnp.where(col <= row, scores(i, hh), -jnp.inf)
        m = jnp.max(s, axis=-1, keepdims=True)
        state.append((m, pv(jnp.exp2(s - m).astype(BF16), i, hh)))

    def body(j, carry):
        ss = [scores(j, hh) for hh in range(2)]
        new = []
        for hh in range(2):
            m, acc = carry[hh]
            m_new = jnp.maximum(m, jnp.max(ss[hh], axis=-1, keepdims=True))
            p = jnp.exp2(ss[hh] - m_new).astype(BF16)
            new.append((m_new, jnp.exp2(m - m_new) * acc + pv(p, j, hh)))
        return tuple(new)

    carry = lax.fori_loop(0, i, body, tuple(state))
    outs = [acc / pltpu.roll(acc, FOX_HEAD_DIM, 1) for _, acc in carry]
    o_ref[0] = jnp.where(_head_mask(outs[0].shape, 0), outs[0], outs[1]).astype(o_ref.dtype)


def _fox_prompt(q, k, v, c, tq):
    bn, s_len, _ = q.shape
    nq = s_len // tq
    c4 = c.reshape(bn, FOX_HEADS * nq, 1, tq)
    qspec = pl.BlockSpec((1, tq, LANES), lambda b, h, i: (b, i, h))
    kspec = pl.BlockSpec((1, s_len, LANES), lambda b, h, i: (b, 0, h))
    return pl.pallas_call(
        functools.partial(_fox_prompt_kernel, tq=tq, nq=nq),
        grid=(bn, FOX_DIM // LANES, nq),
        in_specs=[qspec, kspec, kspec,
                  pl.BlockSpec((1, FOX_HEADS * nq, 1, tq), lambda b, h, i: (b, 0, 0, 0))],
        out_specs=qspec,
        out_shape=jax.ShapeDtypeStruct((bn, s_len, FOX_DIM), BF16),
        scratch_shapes=[pltpu.VMEM((2, s_len, LANES), BF16)],
        compiler_params=_params("parallel", "parallel", "arbitrary"),
        name="fox_prompt",
    )(q, k, v, c4)


def _fox_prompt_t_kernel(q_ref, k_ref, v_ref, c_ref, o_ref, vt_ref, cb_ref, *, tq, nq):
    hp = pl.program_id(1)
    i = pl.program_id(2)
    half = lax.broadcasted_iota(jnp.int32, (LANES, tq), 0) < FOX_HEAD_DIM

    @pl.when(i == 0)
    def _():
        for j in range(nq):
            rows = slice(j * tq, (j + 1) * tq)
            vt = v_ref[0, rows, :].astype(F32).T
            for hh in range(2):
                own = half if hh == 0 else jnp.logical_not(half)
                vt_ref[hh, j] = jnp.where(own, vt, 1.0).astype(BF16)
                c_j = c_ref[0, (hp * 2 + hh) * nq + j]
                rel = (c_j[:, 0:1] - c_j) * LOG2E
                cb_ref[hh, rows, :] = jnp.broadcast_to(rel, (LANES, tq)).T

    q = q_ref[0]
    qh = [jnp.where(_head_mask(q.shape, hh), q, jnp.zeros_like(q)) for hh in range(2)]
    wq = tq // FOX_QUERY_SPLIT
    streams = [(hh, qs) for hh in range(2) for qs in range(FOX_QUERY_SPLIT)]

    def c_start(j, hh):
        return c_ref[0, (hp * 2 + hh) * nq + j][:, 0:1]

    wk = tq // FOX_KEY_SPLIT

    def scores(j, hh, qs, n_keys, k0=0):
        rows = pl.ds(pl.multiple_of(j * tq + k0, wk), n_keys)
        st = _dot_nt(k_ref[0, rows, :], qh[hh][qs * wq:(qs + 1) * wq, :])
        return st + jnp.tile(cb_ref[hh, rows, :], (1, wq // LANES))

    state = []
    for hh, qs in streams:
        n_keys = (qs + 1) * wq
        key = lax.broadcasted_iota(jnp.int32, (n_keys, wq), 0)
        qry = lax.broadcasted_iota(jnp.int32, (n_keys, wq), 1) + qs * wq
        st = jnp.where(key <= qry, scores(i, hh, qs, n_keys), -jnp.inf)
        m = jnp.max(st, axis=0, keepdims=True)
        state.append((m, _dot(vt_ref[hh, i, :, 0:n_keys], jnp.exp2(st - m).astype(BF16))))

    def update(tiles, carry):
        blocks = [(j, kb) for j in tiles for kb in range(FOX_KEY_SPLIT)]
        sts = [[scores(j, hh, qs, wk, kb * wk) for hh, qs in streams] for j, kb in blocks]
        carry = list(carry)
        for (j, kb), st_b in zip(blocks, sts):
            delta = [(c_start(i, hh) - c_start(j, hh)) * LOG2E for hh in range(2)]
            for n, (hh, qs) in enumerate(streams):
                m, acc = carry[n]
                m_new = jnp.maximum(m, jnp.max(st_b[n], axis=0, keepdims=True) + delta[hh])
                p = jnp.exp2(st_b[n] - (m_new - delta[hh])).astype(BF16)
                pv = _dot(vt_ref[hh, j, :, kb * wk:(kb + 1) * wk], p)
                carry[n] = (m_new, jnp.exp2(m - m_new) * acc + pv)
        return tuple(carry)

    carry = lax.fori_loop(
        0, i // FOX_TILES_PER_STEP,
        lambda jj, c: update([jj * FOX_TILES_PER_STEP + t for t in range(FOX_TILES_PER_STEP)], c),
        tuple(state))
    for r in range(1, FOX_TILES_PER_STEP):
        carry = lax.cond(i % FOX_TILES_PER_STEP >= r, lambda c, r=r: update([i - r], c), lambda c: c, carry)
    accs = [jnp.concatenate([carry[hh * FOX_QUERY_SPLIT + qs][1] for qs in range(FOX_QUERY_SPLIT)], axis=1)
            for hh in range(2)]
    out0 = accs[0] / accs[0][FOX_HEAD_DIM:FOX_HEAD_DIM + 1, :]
    out1 = accs[1] / accs[1][0:1, :]
    o_ref[0] = jnp.where(half, out0, out1).T.astype(o_ref.dtype)


def _fox_prompt_t(q, k, v, c, tq):
    bn, s_len, _ = q.shape
    nq = s_len // tq
    c4 = c.reshape(bn, FOX_HEADS * nq, 1, tq)
    qspec = pl.BlockSpec((1, tq, LANES), lambda b, h, i: (b, i, h))
    kspec = pl.BlockSpec((1, s_len, LANES), lambda b, h, i: (b, 0, h))
    return pl.pallas_call(
        functools.partial(_fox_prompt_t_kernel, tq=tq, nq=nq),
        grid=(bn, FOX_DIM // LANES, nq),
        in_specs=[qspec, kspec, kspec,
                  pl.BlockSpec((1, FOX_HEADS * nq, 1, tq), lambda b, h, i: (b, 0, 0, 0))],
        out_specs=qspec,
        out_shape=jax.ShapeDtypeStruct((bn, s_len, FOX_DIM), BF16),
        scratch_shapes=[pltpu.VMEM((2, nq, LANES, tq), BF16),
                        pltpu.VMEM((2, s_len, LANES), F32)],
        compiler_params=_params("parallel", "parallel", "arbitrary"),
        name="fox_prompt",
    )(q, k, v, c4)


def _fox_sample_kernel(q_ref, kn_ref, vn_ref, kc_ref, vc_ref, lf_ref, o_ref, *, t_new, p_len):
    hp = pl.program_id(1)
    q = q_ref[0]
    n_new = lf_ref.shape[-1] - p_len
    zeros = jnp.zeros((n_new - t_new, LANES), BF16)
    kn = jnp.concatenate([kn_ref[0], zeros], axis=0)
    vn = jnp.concatenate([vn_ref[0], zeros], axis=0)
    kc = kc_ref[0].astype(BF16)
    vc = vc_ref[0].astype(BF16)
    row = lax.broadcasted_iota(jnp.int32, (t_new, n_new), 0)
    col = lax.broadcasted_iota(jnp.int32, (t_new, n_new), 1)
    c_all, _ = _lane_prefix_sum(lf_ref[0], jnp.zeros((FOX_HEADS, 1), F32))
    head_row = lax.broadcasted_iota(jnp.int32, c_all.shape, 0)
    outs = []
    for hh in range(2):
        head = hp * 2 + hh
        qh = jnp.where(_head_mask(q.shape, hh), q, jnp.zeros_like(q))
        c_row = jnp.sum(jnp.where(head_row == head, c_all, 0.0), axis=0, keepdims=True)
        bias = (c_row[:, p_len - 1:p_len] - c_row) * LOG2E
        s_c = _dot_nt(qh, kc) + bias[:, 0:p_len]
        s_n = jnp.where(col <= row, _dot_nt(qh, kn) + bias[:, p_len:], -jnp.inf)
        m = jnp.maximum(jnp.max(s_c, axis=-1, keepdims=True), jnp.max(s_n, axis=-1, keepdims=True))
        p_c = jnp.exp2(s_c - m)
        p_n = jnp.exp2(s_n - m)
        l = jnp.sum(p_c, axis=-1, keepdims=True) + jnp.sum(p_n, axis=-1, keepdims=True)
        outs.append((_dot(p_c.astype(BF16), vc) + _dot(p_n.astype(BF16), vn)) / l)
    o_ref[0] = jnp.where(_head_mask(outs[0].shape, 0), outs[0], outs[1]).astype(o_ref.dtype)


def _fox_sample(q, kn, vn, kc, vc, lf_all):
    bn, t_new, _ = q.shape
    p_len = kc.shape[1]
    n_keys = lf_all.shape[-1]
    qspec = pl.BlockSpec((1, t_new, LANES), lambda b, h: (b, 0, h))
    cspec = pl.BlockSpec((1, p_len, LANES), lambda b, h: (b, 0, h))
    return pl.pallas_call(
        functools.partial(_fox_sample_kernel, t_new=t_new, p_len=p_len),
        grid=(bn, FOX_DIM // LANES),
        in_specs=[qspec, qspec, qspec, cspec, cspec,
                  pl.BlockSpec((1, FOX_HEADS, n_keys), lambda b, h: (b, 0, 0))],
        out_specs=qspec,
        out_shape=jax.ShapeDtypeStruct((bn, t_new, FOX_DIM), BF16),
        compiler_params=_params("parallel", "parallel"),
        name="fox_sample",
    )(q, kn, vn, kc, vc, lf_all)


def _gmlp_kernel(x_ref, win_ref, bin_ref, ng_ref, nb_ref, ws_ref, bst_ref, wout_ref,
                 g_ref, b_ref, y_ref, *rest, tm, alpha, emit_vn):
    if emit_vn:
        vn_ref, u_scr, vnb_scr, gated_scr = rest
    else:
        u_scr, vnb_scr, gated_scr = rest
    x = x_ref[...]
    xb = x.astype(BF16)
    u_scr[...] = _gelu(_dot(xb, win_ref[:, 0:GMLP_DIM]) + bin_ref[:, 0:GMLP_DIM])
    v = _gelu(_dot(xb, win_ref[:, GMLP_DIM:]) + bin_ref[:, GMLP_DIM:])
    vn = _layer_norm(v, ng_ref[...], nb_ref[...])
    if emit_vn:
        vn_ref[...] = vn
    vnb_scr[...] = vn.astype(BF16)

    row = lax.broadcasted_iota(jnp.int32, (GMLP_CHUNK, GMLP_CHUNK), 0)
    col = lax.broadcasted_iota(jnp.int32, (GMLP_CHUNK, GMLP_CHUNK), 1)
    gd = GMLP_DIM // GMLP_GROUPS
    n_chunks = tm // GMLP_CHUNK
    for g in range(GMLP_GROUPS):
        wm = jnp.where(col <= row, ws_ref[g], 0.0).astype(BF16)
        bias = bst_ref[:, g:g + 1]
        cs = slice(g * gd, (g + 1) * gd)
        for r in range(0, n_chunks, 2):
            rss = [slice(rr * GMLP_CHUNK, (rr + 1) * GMLP_CHUNK) for rr in range(r, min(r + 2, n_chunks))]
            mixed = _dot(wm, jnp.concatenate([vnb_scr[rs, cs] for rs in rss], axis=1)) + bias
            for n, rs in enumerate(rss):
                gated_scr[rs, cs] = (u_scr[rs, cs] * mixed[:, n * gd:(n + 1) * gd]).astype(BF16)
    out = _dot(gated_scr[...], wout_ref[...])
    y_ref[...] = _layer_norm(alpha * x + out, g_ref[...], b_ref[...])


def _gmlp_mixer(x, layer, win, bin_, ng, nb, ws, bst, wout, g, b, alpha, tm, emit_vn):
    n = x.shape[0]
    tok = pl.BlockSpec((tm, D_MODEL), lambda i: (i, 0))
    out_shape = [jax.ShapeDtypeStruct((n, D_MODEL), F32)]
    out_specs = [tok]
    if emit_vn:
        out_shape.append(jax.ShapeDtypeStruct((n, GMLP_DIM), F32))
        out_specs.append(tok)
    res = pl.pallas_call(
        functools.partial(_gmlp_kernel, tm=tm, alpha=alpha, emit_vn=emit_vn),
        grid=(n // tm,),
        in_specs=[tok, _resident(win.shape, layer), _resident(bin_.shape), _resident(ng.shape),
                  _resident(nb.shape), _resident(ws.shape, layer), _resident(bst.shape),
                  _resident(wout.shape, layer), _resident(g.shape), _resident(b.shape)],
        out_specs=tuple(out_specs),
        out_shape=tuple(out_shape),
        scratch_shapes=[pltpu.VMEM((tm, GMLP_DIM), F32), pltpu.VMEM((tm, GMLP_DIM), BF16),
                        pltpu.VMEM((tm, GMLP_DIM), BF16)],
        compiler_params=_params("parallel"),
        name="gmlp_mixer",
    )(x, win, bin_, ng, nb, ws, bst, wout, g, b)
    return res if emit_vn else (res[0], None)


def _mem_kv_kernel(m_ref, wk_ref, wv_ref, k_ref, v_ref):
    mb = m_ref[0].astype(BF16)
    k = _dot(mb, wk_ref[0])
    v = _dot(mb, wv_ref[0])
    for h in range(MEM_HEADS):
        hs = slice(h * MEM_HEAD_DIM, (h + 1) * MEM_HEAD_DIM)
        k_ref[0, 0, :, h, :] = k[:, hs]
        v_ref[0, 0, :, h, :] = v[:, hs]


def _mem_kv(mem, wk, wv):
    bn, mlen, _ = mem.shape
    depth = wk.shape[0]
    wspec = pl.BlockSpec((1, D_MODEL, MEM_DIM), lambda l, b: (l, 0, 0))
    ospec = pl.BlockSpec((1, 1, mlen, MEM_HEADS, MEM_HEAD_DIM), lambda l, b: (l, b, 0, 0, 0))
    shape = jax.ShapeDtypeStruct((depth, bn, mlen, MEM_HEADS, MEM_HEAD_DIM), F32)
    return pl.pallas_call(
        _mem_kv_kernel,
        grid=(depth, bn),
        in_specs=[pl.BlockSpec((1, mlen, D_MODEL), lambda l, b: (b, 0, 0)), wspec, wspec],
        out_specs=(ospec, ospec), out_shape=(shape, shape),
        compiler_params=_params("parallel", "parallel"),
        name="mem_kv",
    )(mem, wk, wv)


def _mem_attn_kernel(x_ref, *refs, alpha, fuse_even):
    if fuse_even:
        att_ref, gc_ref, wmix_ref, g0_ref, b0_ref = refs[:5]
        refs = refs[5:]
    mk_ref, mv_ref, wq_ref, wo_ref, g_ref, b_ref, y_ref, mkb_ref, mvb_ref = refs

    @pl.when(pl.program_id(1) == 0)
    def _():
        for h in range(MEM_HEADS):
            mkb_ref[h] = mk_ref[0, 0, :, h, :].astype(BF16)
            mvb_ref[h] = mv_ref[0, 0, :, h, :].astype(BF16)

    x = x_ref[0]
    if fuse_even:
        mix = _dot(att_ref[0], wmix_ref[0:FOX_DIM, :]) + _dot(gc_ref[0], wmix_ref[FOX_DIM:, :])
        x = _layer_norm(alpha * x + mix, g0_ref[...], b0_ref[...])
    q = _dot(x.astype(BF16), wq_ref[...]) * (LOG2E * MEM_HEAD_DIM ** -0.5)
    qb = q.astype(BF16)
    heads = []
    for h in range(MEM_HEADS):
        hs = slice(h * MEM_HEAD_DIM, (h + 1) * MEM_HEAD_DIM)
        mk = mkb_ref[h]
        mv = mvb_ref[h]
        s = _dot_nt(qb[:, hs], mk)
        m = jnp.max(s, axis=-1, keepdims=True)
        p = jnp.exp2(s - m)
        l = jnp.sum(p, axis=-1, keepdims=True)
        heads.append((_dot(p.astype(BF16), mv) / l).astype(BF16))
    o = jnp.concatenate(heads, axis=-1)
    y_ref[0] = _layer_norm(alpha * x + _dot(o, wo_ref[...]), g_ref[...], b_ref[...])


def _mem_attn(x, mk, mv, layer, wq, wo, g, b, alpha, tm, even=None):
    bn, tlen, _ = x.shape
    mlen = mk.shape[2]
    tok = lambda w: pl.BlockSpec((1, tm, w), lambda bb, t: (bb, t, 0))
    mspec = pl.BlockSpec((1, 1, mlen, MEM_HEADS, MEM_HEAD_DIM), lambda bb, t: (layer, bb, 0, 0, 0))
    pre_args, pre_specs = [], []
    if even is not None:
        att, gc, wmix, e, g0, b0 = even
        pre_args = [att, gc, wmix, g0, b0]
        pre_specs = [tok(FOX_DIM), tok(CONV_DIM), _resident(wmix.shape, e), _resident(g0.shape),
                     _resident(b0.shape)]
    return pl.pallas_call(
        functools.partial(_mem_attn_kernel, alpha=alpha, fuse_even=even is not None),
        grid=(bn, tlen // tm),
        in_specs=[tok(D_MODEL)] + pre_specs + [mspec, mspec, _resident(wq.shape, layer),
                                               _resident(wo.shape, layer), _resident(g.shape),
                                               _resident(b.shape)],
        out_specs=tok(D_MODEL),
        out_shape=jax.ShapeDtypeStruct((bn, tlen, D_MODEL), F32),
        scratch_shapes=[pltpu.VMEM((MEM_HEADS, mlen, MEM_HEAD_DIM), BF16),
                        pltpu.VMEM((MEM_HEADS, mlen, MEM_HEAD_DIM), BF16)],
        compiler_params=_params("parallel", "arbitrary"),
        name="mem_attn",
    )(x, *pre_args, mk, mv, wq, wo, g, b)


def _ffn_kernel(x_ref, wg_ref, wu_ref, wd_ref, g_ref, b_ref, y_ref, h_scr, *, alpha):
    x = x_ref[...]
    xb = x.astype(BF16)
    d_ff = wg_ref.shape[1]
    for c in range(d_ff // FFN_CHUNK):
        cols = slice(c * FFN_CHUNK, (c + 1) * FFN_CHUNK)
        gate = _dot(xb, wg_ref[:, cols])
        up = _dot(xb, wu_ref[:, cols])
        h_scr[:, cols] = (gate * jax.nn.sigmoid(gate) * up).astype(BF16)
    y = alpha * x + _dot(h_scr[...], wd_ref[...])
    y_ref[...] = _layer_norm(y, g_ref[...], b_ref[...])


def _ffn(x, layer, wg, wu, wd, g, b, alpha, tm):
    n = x.shape[0]
    d_ff = wg.shape[-1]
    assert d_ff % FFN_CHUNK == 0
    tok = pl.BlockSpec((tm, D_MODEL), lambda i: (i, 0))
    return pl.pallas_call(
        functools.partial(_ffn_kernel, alpha=alpha),
        grid=(n // tm,),
        in_specs=[tok, _resident(wg.shape, layer), _resident(wu.shape, layer), _resident(wd.shape, layer),
                  _resident(g.shape), _resident(b.shape)],
        out_specs=tok,
        out_shape=jax.ShapeDtypeStruct((n, D_MODEL), F32),
        scratch_shapes=[pltpu.VMEM((tm, d_ff), BF16)],
        compiler_params=_params("parallel"),
        name="swiglu_ffn",
    )(x, wg, wu, wd, g, b)


def kernel(x_prompt, x_sample, cache_fox_k, cache_fox_v, cache_fox_logf, state_conv, cache_mem_k, cache_mem_v, mem_prompt, w_in_even, b_forget, conv_w, w_out_even, w_in_odd, b_in_odd, gmlp_norm_g, gmlp_norm_b, gmlp_w_s, gmlp_b_s, w_out_odd, mem_w_q, mem_w_k, mem_w_v, mem_w_o, ffn_w_gate, ffn_w_up, ffn_w_down, ln_g, ln_b):
    depth = ln_g.shape[0]
    alpha = float((2 * depth) ** 0.25)
    bp, s_len, d = x_prompt.shape
    bs, t_new, _ = x_sample.shape
    p_len = cache_fox_k.shape[2]
    row2 = lambda a: a.reshape(1, -1)

    w_out_even_b, w_in_odd_b, w_out_odd_b = (w.astype(BF16) for w in (w_out_even, w_in_odd, w_out_odd))
    mem_wq_b, mem_wo_b = mem_w_q.astype(BF16), mem_w_o.astype(BF16)
    ffn_wg_b, ffn_wu_b, ffn_wd_b = (w.astype(BF16) for w in (ffn_w_gate, ffn_w_up, ffn_w_down))

    mk_p, mv_p = _mem_kv(mem_prompt, mem_w_k.astype(BF16), mem_w_v.astype(BF16))

    yp, ys = x_prompt, x_sample
    fk_p, fv_p, fl_p, cs_p = [], [], [], []
    fk_s, fv_s, fl_s, cs_s, gv_s = [], [], [], [], []
    for layer in range(depth):
        lg, lb = ln_g[layer], ln_b[layer]
        even_p = even_s = None
        if layer % 2 == 0:
            e = layer // 2
            w_in = w_in_even[e].astype(BF16)
            qkv_w = w_in[:, 0:3 * FOX_DIM]
            f_wt = w_in[:, 3 * FOX_DIM:3 * FOX_DIM + FOX_HEADS].T
            c_w = w_in[:, 3 * FOX_DIM + FOX_HEADS:]
            f_b = b_forget[e].reshape(FOX_HEADS, 1)
            cw = conv_w[e]

            st0 = jnp.zeros((bp, SUBLANES, CONV_DIM), F32)
            q, kf, vf, kb, vb, lf, gc, cst, c = _even_inproj(yp, qkv_w, f_wt, c_w, f_b, cw, st0,
                                                             TOKEN_TILE, True)
            att = _fox_prompt_t(q, kb, vb, c, FOX_TILE)
            even_p = (att, gc, w_out_even_b, e, row2(lg[0]), row2(lb[0]))
            fk_p.append(kf.reshape(bp, s_len, FOX_HEADS, FOX_HEAD_DIM))
            fv_p.append(vf.reshape(bp, s_len, FOX_HEADS, FOX_HEAD_DIM))
            fl_p.append(jnp.swapaxes(lf, 1, 2))
            cs_p.append(cst)

            st = jnp.pad(state_conv[e], ((0, 0), (SUBLANES - (CONV_WIDTH - 1), 0), (0, 0)))
            q, kf, vf, kb, vb, lf, gc, cst = _even_inproj(ys, qkv_w, f_wt, c_w, f_b, cw, st, t_new, False)
            pad = -(p_len + t_new) % LANES
            lf_all = jnp.concatenate([jnp.swapaxes(cache_fox_logf[e], 1, 2), lf,
                                      jnp.zeros((bs, FOX_HEADS, pad), F32)], axis=2)
            att = _fox_sample(q, kb, vb, cache_fox_k[e].reshape(bs, p_len, FOX_DIM),
                              cache_fox_v[e].reshape(bs, p_len, FOX_DIM), lf_all)
            even_s = (att, gc, w_out_even_b, e, row2(lg[0]), row2(lb[0]))
            fk_s.append(kf.reshape(bs, t_new, FOX_HEADS, FOX_HEAD_DIM))
            fv_s.append(vf.reshape(bs, t_new, FOX_HEADS, FOX_HEAD_DIM))
            fl_s.append(jnp.swapaxes(lf, 1, 2))
            cs_s.append(cst)
        else:
            o = layer // 2
            args = (o, w_in_odd_b, row2(b_in_odd[o]), row2(gmlp_norm_g[o]), row2(gmlp_norm_b[o]),
                    gmlp_w_s, gmlp_b_s[o].T, w_out_odd_b, row2(lg[0]), row2(lb[0]), alpha)
            yp, _ = _gmlp_mixer(yp.reshape(-1, d), *args, TOKEN_TILE, False)
            yp = yp.reshape(bp, s_len, d)
            ys_pad = jnp.pad(ys, ((0, 0), (0, GMLP_CHUNK - t_new), (0, 0))).reshape(-1, d)
            ys_pad, vn = _gmlp_mixer(ys_pad, *args, TOKEN_TILE, True)
            ys = ys_pad.reshape(bs, GMLP_CHUNK, d)[:, :t_new]
            gv_s.append(vn.reshape(bs, GMLP_CHUNK, GMLP_DIM)[:, :t_new])

        yp = _mem_attn(yp, mk_p, mv_p, layer, mem_wq_b, mem_wo_b, row2(lg[1]), row2(lb[1]), alpha,
                       TOKEN_TILE, even_p)
        ys = _mem_attn(ys, cache_mem_k, cache_mem_v, layer, mem_wq_b, mem_wo_b, row2(lg[1]), row2(lb[1]),
                       alpha, t_new, even_s)

        ffn_args = (layer, ffn_wg_b, ffn_wu_b, ffn_wd_b, row2(lg[2]), row2(lb[2]), alpha)
        yp = _ffn(yp.reshape(-1, d), *ffn_args, TOKEN_TILE).reshape(bp, s_len, d)
        ys = _ffn(ys.reshape(-1, d), *ffn_args, bs * t_new).reshape(bs, t_new, d)

    return (yp, ys, jnp.stack(fk_p), jnp.stack(fv_p), jnp.stack(fl_p), jnp.stack(cs_p), mk_p, mv_p,
            jnp.stack(fk_s), jnp.stack(fv_s), jnp.stack(fl_s), jnp.stack(cs_s), jnp.stack(gv_s))
```

```python
import functools

import jax
import jax.numpy as jnp
from jax import lax
from jax.experimental import pallas as pl
from jax.experimental.pallas import tpu as pltpu

F32 = jnp.float32
BF16 = jnp.bfloat16

D_MODEL = 1024
FOX_HEADS = 8
FOX_HEAD_DIM = 64
FOX_DIM = FOX_HEADS * FOX_HEAD_DIM
CONV_DIM = 512
CONV_WIDTH = 3
GMLP_CHUNK = 128
GMLP_GROUPS = 8
GMLP_DIM = D_MODEL
MEM_HEADS = 4
MEM_HEAD_DIM = 128
MEM_DIM = MEM_HEADS * MEM_HEAD_DIM
LN_EPS = 1e-5

LANES = 128
SUBLANES = 8
VMEM_LIMIT_BYTES = 56 * 1024 * 1024
TOKEN_TILE = 512
FOX_TILE = 512
FOX_KEY_SPLIT = 2
FFN_CHUNK = 256
FFN_ROWS_PER_STEP = 2 * TOKEN_TILE
LOG2E = 1.4426950408889634


def _params(*semantics):
    return pltpu.CompilerParams(dimension_semantics=semantics,
                                vmem_limit_bytes=VMEM_LIMIT_BYTES)


def _resident(shape, layer=None):
    if layer is None:
        nd = len(shape)
        return pl.BlockSpec(shape, lambda *_: (0,) * nd, pipeline_mode=pl.Buffered(1))
    nd = len(shape) - 1
    return pl.BlockSpec((None,) + tuple(shape[1:]), lambda *_: (layer,) + (0,) * nd,
                        pipeline_mode=pl.Buffered(1))


def _dot(a, b):
    return jnp.dot(a, b, preferred_element_type=F32)


def _dot_nt(a, b):
    return lax.dot_general(a, b, (((1,), (1,)), ((), ())), preferred_element_type=F32)


def _layer_norm(y, g, b):
    mu = jnp.mean(y, axis=-1, keepdims=True)
    d = y - mu
    var = jnp.mean(d * d, axis=-1, keepdims=True)
    return d * lax.rsqrt(var + LN_EPS) * g + b


def _gelu(z):
    return 0.5 * z * (1.0 + lax.erf(z * (2.0 ** -0.5)))


def _lane_prefix_sum(x, carry):
    rows, n = x.shape
    lane = lax.broadcasted_iota(jnp.int32, (rows, LANES), 1)
    segs = []
    for j in range(n // LANES):
        seg = x[:, j * LANES:(j + 1) * LANES]
        step = 1
        while step < LANES:
            seg = seg + jnp.where(lane >= step, pltpu.roll(seg, step, 1), 0.0)
            step *= 2
        segs.append(seg)
    outs = []
    for seg in segs:
        seg = seg + carry
        outs.append(seg)
        carry = seg[:, LANES - 1:LANES]
    return jnp.concatenate(outs, axis=1), carry


def _even_inproj_kernel(x_ref, wqkv_ref, wft_ref, wc_ref, bf_ref, cw_ref, st_ref,
                        q_ref, kf_ref, vf_ref, kb_ref, vb_ref, lf_ref, gc_ref, cs_ref, *rest,
                        tm, nt, emit_c):
    if emit_c:
        c_ref, prev_ref, ccarry_ref = rest
    else:
        (prev_ref,) = rest
    t = pl.program_id(1)

    @pl.when(t == 0)
    def _():
        prev_ref[...] = st_ref[0]
        if emit_c:
            ccarry_ref[...] = jnp.zeros_like(ccarry_ref)

    xb = x_ref[0].astype(BF16)
    lf = jax.nn.log_sigmoid(_dot_nt(wft_ref[...], xb) + bf_ref[...])
    lf_ref[0] = lf
    if emit_c:
        c, carry = _lane_prefix_sum(lf, ccarry_ref[...])
        c_ref[0] = c
        ccarry_ref[...] = carry

    h = _dot(xb, wc_ref[:, 0:CONV_DIM])
    bg = _dot(xb, wc_ref[:, CONV_DIM:2 * CONV_DIM])
    cg = _dot(xb, wc_ref[:, 2 * CONV_DIM:3 * CONV_DIM])
    pre = cg * h

    prev = prev_ref[...]
    row = lax.broadcasted_iota(jnp.int32, (tm, CONV_DIM), 0)
    p1 = jnp.where(row == 0, prev[7:8, :], pltpu.roll(pre, 1, 0))
    p2 = jnp.where(row == 0, prev[6:7, :],
                   jnp.where(row == 1, prev[7:8, :], pltpu.roll(pre, 2, 0)))
    cw = cw_ref[...]
    conv = cw[0:1, :] * p2 + cw[1:2, :] * p1 + cw[2:3, :] * pre
    gc_ref[0] = (bg * conv).astype(BF16)
    prev_ref[...] = pre[tm - SUBLANES:tm, :]

    q = _dot(xb, wqkv_ref[:, 0:FOX_DIM])
    q_ref[0] = (q * (LOG2E * FOX_HEAD_DIM ** -0.5)).astype(BF16)
    k = _dot(xb, wqkv_ref[:, FOX_DIM:2 * FOX_DIM])
    kf_ref[0] = k
    kb_ref[0] = k.astype(BF16)
    v = _dot(xb, wqkv_ref[:, 2 * FOX_DIM:3 * FOX_DIM])
    vf_ref[0] = v
    vb_ref[0] = v.astype(BF16)

    @pl.when(t == nt - 1)
    def _():
        cs_ref[0] = pre[tm - (CONV_WIDTH - 1):tm, :]


def _even_inproj(x, wqkv, wft, wc, bf, cw, state8, tm, emit_c):
    bn, tlen, _ = x.shape
    nt = tlen // tm
    tok = lambda w: pl.BlockSpec((1, tm, w), lambda b, t: (b, t, 0))
    tlane = pl.BlockSpec((1, FOX_HEADS, tm), lambda b, t: (b, 0, t))
    out_shape = [
        jax.ShapeDtypeStruct((bn, tlen, FOX_DIM), BF16),
        jax.ShapeDtypeStruct((bn, tlen, FOX_DIM), F32),
        jax.ShapeDtypeStruct((bn, tlen, FOX_DIM), F32),
        jax.ShapeDtypeStruct((bn, tlen, FOX_DIM), BF16),
        jax.ShapeDtypeStruct((bn, tlen, FOX_DIM), BF16),
        jax.ShapeDtypeStruct((bn, FOX_HEADS, tlen), F32),
        jax.ShapeDtypeStruct((bn, tlen, CONV_DIM), BF16),
        jax.ShapeDtypeStruct((bn, CONV_WIDTH - 1, CONV_DIM), F32),
    ]
    out_specs = [tok(FOX_DIM), tok(FOX_DIM), tok(FOX_DIM), tok(FOX_DIM), tok(FOX_DIM), tlane,
                 tok(CONV_DIM), pl.BlockSpec((1, CONV_WIDTH - 1, CONV_DIM), lambda b, t: (b, 0, 0))]
    scratch = [pltpu.VMEM((SUBLANES, CONV_DIM), F32)]
    if emit_c:
        out_shape.append(jax.ShapeDtypeStruct((bn, FOX_HEADS, tlen), F32))
        out_specs.append(tlane)
        scratch.append(pltpu.VMEM((FOX_HEADS, 1), F32))
    return pl.pallas_call(
        functools.partial(_even_inproj_kernel, tm=tm, nt=nt, emit_c=emit_c),
        grid=(bn, nt),
        in_specs=[tok(D_MODEL), _resident(wqkv.shape), _resident(wft.shape),
                  _resident(wc.shape), _resident(bf.shape), _resident(cw.shape),
                  pl.BlockSpec((1, SUBLANES, CONV_DIM), lambda b, t: (b, 0, 0))],
        out_specs=tuple(out_specs),
        out_shape=tuple(out_shape),
        scratch_shapes=scratch,
        compiler_params=_params("parallel", "arbitrary"),
        name="even_inproj",
    )(x, wqkv, wft, wc, bf, cw, state8)


def _head_mask(shape, hh):
    lane = lax.broadcasted_iota(jnp.int32, shape, 1)
    return (lane >= FOX_HEAD_DIM) if hh == 1 else (lane < FOX_HEAD_DIM)


def _fox_prompt_kernel(q_ref, k_ref, v_ref, c_ref, o_ref, vt_ref, cb_ref, *, tq, nq):
    hp = pl.program_id(1)
    i = pl.program_id(2)
    half = lax.broadcasted_iota(jnp.int32, (LANES, tq), 0) < FOX_HEAD_DIM

    @pl.when(i == 0)
    def _():
        for j in range(nq):
            rows = slice(j * tq, (j + 1) * tq)
            vt = v_ref[0, rows, :].astype(F32).T
            for hh in range(2):
                own = half if hh == 0 else jnp.logical_not(half)
                vt_ref[hh, j] = jnp.where(own, vt, 1.0).astype(BF16)
                c_j = c_ref[0, (hp * 2 + hh) * nq + j]
                rel = (c_j[:, 0:1] - c_j) * LOG2E
                cb_ref[hh, rows, :] = jnp.broadcast_to(rel, (LANES, tq)).T

    q = q_ref[0]
    qh = [jnp.where(_head_mask(q.shape, hh), q, jnp.zeros_like(q)) for hh in range(2)]
    wk = tq // FOX_KEY_SPLIT
    key = lax.broadcasted_iota(jnp.int32, (wk, tq), 0)
    qry = lax.broadcasted_iota(jnp.int32, (wk, tq), 1)

    def c_start(j, hh):
        return c_ref[0, (hp * 2 + hh) * nq + j][:, 0:1]

    def scores(j, kb, hh):
        rows = pl.ds(pl.multiple_of(j * tq + kb * wk, wk), wk)
        st = _dot_nt(k_ref[0, rows, :], qh[hh])
        return st + jnp.tile(cb_ref[hh, rows, :], (1, tq // LANES))

    def chain(tiles, carry, diagonal=False):
        blocks = [(i, kb, True) for kb in range(FOX_KEY_SPLIT)] if diagonal else []
        blocks += [(j, kb, False) for j in tiles for kb in range(FOX_KEY_SPLIT)]
        sts = [[scores(j, kb, hh) for hh in range(2)] for j, kb, _ in blocks]
        carry = [None, None] if carry is None else list(carry)
        for (j, kb, diag), st_b in zip(blocks, sts):
            for hh in range(2):
                st = st_b[hh]
                if diag:
                    st = jnp.where(key + kb * wk <= qry, st, -jnp.inf)
                    top = jnp.max(st, axis=0, keepdims=True)
                else:
                    delta = (c_start(i, hh) - c_start(j, hh)) * LOG2E
                    top = jnp.max(st, axis=0, keepdims=True) + delta
                m_new = top if carry[hh] is None else jnp.maximum(carry[hh][0], top)
                p = jnp.exp2(st - (m_new if diag else m_new - delta)).astype(BF16)
                pv = _dot(vt_ref[hh, j, :, kb * wk:(kb + 1) * wk], p)
                if carry[hh] is None:
                    carry[hh] = (m_new, pv)
                else:
                    carry[hh] = (m_new, jnp.exp2(carry[hh][0] - m_new) * carry[hh][1] + pv)
        return tuple(carry)

    carry = lax.cond(i % 2 == 1, lambda: chain([i - 1], None, True), lambda: chain([], None, True))
    carry = lax.fori_loop(0, i // 2, lambda jj, c: chain([2 * jj, 2 * jj + 1], c), carry)
    out0 = carry[0][1] / carry[0][1][FOX_HEAD_DIM:FOX_HEAD_DIM + 1, :]
    out1 = carry[1][1] / carry[1][1][0:1, :]
    o_ref[0] = jnp.where(half, out0, out1).T.astype(o_ref.dtype)


def _fox_prompt(q, k, v, c, tq):
    bn, s_len, _ = q.shape
    nq = s_len // tq
    c4 = c.reshape(bn, FOX_HEADS * nq, 1, tq)
    qspec = pl.BlockSpec((1, tq, LANES), lambda b, h, i: (b, i, h))
    kspec = pl.BlockSpec((1, s_len, LANES), lambda b, h, i: (b, 0, h))
    return pl.pallas_call(
        functools.partial(_fox_prompt_kernel, tq=tq, nq=nq),
        grid=(bn, FOX_DIM // LANES, nq),
        in_specs=[qspec, kspec, kspec,
                  pl.BlockSpec((1, FOX_HEADS * nq, 1, tq), lambda b, h, i: (b, 0, 0, 0))],
        out_specs=qspec,
        out_shape=jax.ShapeDtypeStruct((bn, s_len, FOX_DIM), BF16),
        scratch_shapes=[pltpu.VMEM((2, nq, LANES, tq), BF16),
                        pltpu.VMEM((2, s_len, LANES), F32)],
        compiler_params=_params("parallel", "parallel", "arbitrary"),
        name="fox_prompt",
    )(q, k, v, c4)


def _fox_sample_kernel(q_ref, kn_ref, vn_ref, kc_ref, vc_ref, lf_ref, o_ref, *, t_new, p_len):
    hp = pl.program_id(1)
    q = q_ref[0]
    n_new = lf_ref.shape[-1] - p_len
    zeros = jnp.zeros((n_new - t_new, LANES), BF16)
    kn = jnp.concatenate([kn_ref[0], zeros], axis=0)
    vn = jnp.concatenate([vn_ref[0], zeros], axis=0)
    kc = kc_ref[0].astype(BF16)
    vc = vc_ref[0].astype(BF16)
    row = lax.broadcasted_iota(jnp.int32, (t_new, n_new), 0)
    col = lax.broadcasted_iota(jnp.int32, (t_new, n_new), 1)
    c_all, _ = _lane_prefix_sum(lf_ref[0], jnp.zeros((FOX_HEADS, 1), F32))
    head_row = lax.broadcasted_iota(jnp.int32, c_all.shape, 0)
    outs = []
    for hh in range(2):
        head = hp * 2 + hh
        qh = jnp.where(_head_mask(q.shape, hh), q, jnp.zeros_like(q))
        c_row = jnp.sum(jnp.where(head_row == head, c_all, 0.0), axis=0, keepdims=True)
        bias = (c_row[:, p_len - 1:p_len] - c_row) * LOG2E
        s_c = _dot_nt(qh, kc) + bias[:, 0:p_len]
        s_n = jnp.where(col <= row, _dot_nt(qh, kn) + bias[:, p_len:], -jnp.inf)
        m = jnp.maximum(jnp.max(s_c, axis=-1, keepdims=True), jnp.max(s_n, axis=-1, keepdims=True))
        p_c = jnp.exp2(s_c - m)
        p_n = jnp.exp2(s_n - m)
        l = jnp.sum(p_c, axis=-1, keepdims=True) + jnp.sum(p_n, axis=-1, keepdims=True)
        outs.append((_dot(p_c.astype(BF16), vc) + _dot(p_n.astype(BF16), vn)) / l)
    o_ref[0] = jnp.where(_head_mask(outs[0].shape, 0), outs[0], outs[1]).astype(o_ref.dtype)


def _fox_sample(q, kn, vn, kc, vc, lf_all):
    bn, t_new, _ = q.shape
    p_len = kc.shape[1]
    n_keys = lf_all.shape[-1]
    qspec = pl.BlockSpec((1, t_new, LANES), lambda b, h: (b, 0, h))
    cspec = pl.BlockSpec((1, p_len, LANES), lambda b, h: (b, 0, h))
    return pl.pallas_call(
        functools.partial(_fox_sample_kernel, t_new=t_new, p_len=p_len),
        grid=(bn, FOX_DIM // LANES),
        in_specs=[qspec, qspec, qspec, cspec, cspec,
                  pl.BlockSpec((1, FOX_HEADS, n_keys), lambda b, h: (b, 0, 0))],
        out_specs=qspec,
        out_shape=jax.ShapeDtypeStruct((bn, t_new, FOX_DIM), BF16),
        compiler_params=_params("parallel", "parallel"),
        name="fox_sample",
    )(q, kn, vn, kc, vc, lf_all)


def _gmlp_kernel(x_ref, win_ref, bin_ref, ng_ref, nb_ref, ws_ref, bst_ref, wout_ref,
                 g_ref, b_ref, y_ref, *rest, tm, alpha, emit_vn):
    if emit_vn:
        vn_ref, u_scr, vnb_scr, gated_scr = rest
    else:
        u_scr, vnb_scr, gated_scr = rest
    x = x_ref[...]
    xb = x.astype(BF16)
    v = _gelu(_dot(xb, win_ref[:, GMLP_DIM:]) + bin_ref[:, GMLP_DIM:])
    vn = _layer_norm(v, ng_ref[...], nb_ref[...])
    if emit_vn:
        vn_ref[...] = vn
    vnb_scr[...] = vn.astype(BF16)
    u_scr[...] = _gelu(_dot(xb, win_ref[:, 0:GMLP_DIM]) + bin_ref[:, 0:GMLP_DIM])

    row = lax.broadcasted_iota(jnp.int32, (GMLP_CHUNK, GMLP_CHUNK), 0)
    col = lax.broadcasted_iota(jnp.int32, (GMLP_CHUNK, GMLP_CHUNK), 1)
    gd = GMLP_DIM // GMLP_GROUPS
    n_chunks = tm // GMLP_CHUNK
    for g in range(GMLP_GROUPS):
        wm = jnp.where(col <= row, ws_ref[g], 0.0).astype(BF16)
        bias = bst_ref[:, g:g + 1]
        cs = slice(g * gd, (g + 1) * gd)
        for r in range(0, n_chunks, 2):
            rss = [slice(rr * GMLP_CHUNK, (rr + 1) * GMLP_CHUNK) for rr in range(r, min(r + 2, n_chunks))]
            mixed = _dot(wm, jnp.concatenate([vnb_scr[rs, cs] for rs in rss], axis=1)) + bias
            for n, rs in enumerate(rss):
                gated_scr[rs, cs] = (u_scr[rs, cs] * mixed[:, n * gd:(n + 1) * gd]).astype(BF16)
    out = _dot(gated_scr[...], wout_ref[...])
    y_ref[...] = _layer_norm(alpha * x + out, g_ref[...], b_ref[...])


def _gmlp_mixer(x, layer, win, bin_, ng, nb, ws, bst, wout, g, b, alpha, tm, emit_vn):
    n = x.shape[0]
    tok = pl.BlockSpec((tm, D_MODEL), lambda i: (i, 0))
    out_shape = [jax.ShapeDtypeStruct((n, D_MODEL), F32)]
    out_specs = [tok]
    if emit_vn:
        out_shape.append(jax.ShapeDtypeStruct((n, GMLP_DIM), F32))
        out_specs.append(tok)
    res = pl.pallas_call(
        functools.partial(_gmlp_kernel, tm=tm, alpha=alpha, emit_vn=emit_vn),
        grid=(n // tm,),
        in_specs=[tok, _resident(win.shape, layer), _resident(bin_.shape), _resident(ng.shape),
                  _resident(nb.shape), _resident(ws.shape, layer), _resident(bst.shape),
                  _resident(wout.shape, layer), _resident(g.shape), _resident(b.shape)],
        out_specs=tuple(out_specs),
        out_shape=tuple(out_shape),
        scratch_shapes=[pltpu.VMEM((tm, GMLP_DIM), F32), pltpu.VMEM((tm, GMLP_DIM), BF16),
                        pltpu.VMEM((tm, GMLP_DIM), BF16)],
        compiler_params=_params("parallel"),
        name="gmlp_mixer",
    )(x, win, bin_, ng, nb, ws, bst, wout, g, b)
    return res if emit_vn else (res[0], None)


def _mem_kv_kernel(m_ref, wk_ref, wv_ref, k_ref, v_ref):
    mb = m_ref[0].astype(BF16)
    k = _dot(mb, wk_ref[0])
    v = _dot(mb, wv_ref[0])
    for h in range(MEM_HEADS):
        hs = slice(h * MEM_HEAD_DIM, (h + 1) * MEM_HEAD_DIM)
        k_ref[0, 0, :, h, :] = k[:, hs]
        v_ref[0, 0, :, h, :] = v[:, hs]


def _mem_kv(mem, wk, wv):
    bn, mlen, _ = mem.shape
    depth = wk.shape[0]
    wspec = pl.BlockSpec((1, D_MODEL, MEM_DIM), lambda l, b: (l, 0, 0))
    ospec = pl.BlockSpec((1, 1, mlen, MEM_HEADS, MEM_HEAD_DIM), lambda l, b: (l, b, 0, 0, 0))
    shape = jax.ShapeDtypeStruct((depth, bn, mlen, MEM_HEADS, MEM_HEAD_DIM), F32)
    return pl.pallas_call(
        _mem_kv_kernel,
        grid=(depth, bn),
        in_specs=[pl.BlockSpec((1, mlen, D_MODEL), lambda l, b: (b, 0, 0)), wspec, wspec],
        out_specs=(ospec, ospec), out_shape=(shape, shape),
        compiler_params=_params("parallel", "parallel"),
        name="mem_kv",
    )(mem, wk, wv)


def _mem_attn_kernel(x_ref, *refs, alpha, fuse_even, n_sub):
    if fuse_even:
        att_ref, gc_ref, wmix_ref, g0_ref, b0_ref = refs[:5]
        refs = refs[5:]
    mk_ref, mv_ref, wq_ref, wo_ref, g_ref, b_ref, y_ref, mkb_ref, mvb_ref = refs

    @pl.when(pl.program_id(1) == 0)
    def _():
        for h in range(MEM_HEADS):
            mkb_ref[h] = mk_ref[0, 0, :, h, :].astype(BF16)
            mvb_ref[h] = mv_ref[0, 0, :, h, :].astype(BF16)

    rows_per = x_ref.shape[1] // n_sub
    for sb in range(n_sub):
        rows = slice(sb * rows_per, (sb + 1) * rows_per)
        x = x_ref[0, rows, :]
        if fuse_even:
            mix = _dot(att_ref[0, rows, :], wmix_ref[0:FOX_DIM, :]) + _dot(gc_ref[0, rows, :], wmix_ref[FOX_DIM:, :])
            x = _layer_norm(alpha * x + mix, g0_ref[...], b0_ref[...])
        q = _dot(x.astype(BF16), wq_ref[...]) * (LOG2E * MEM_HEAD_DIM ** -0.5)
        qb = q.astype(BF16)
        heads = []
        for h in range(MEM_HEADS):
            hs = slice(h * MEM_HEAD_DIM, (h + 1) * MEM_HEAD_DIM)
            s = _dot_nt(qb[:, hs], mkb_ref[h])
            m = jnp.max(s, axis=-1, keepdims=True)
            p = jnp.exp2(s - m)
            l = jnp.sum(p, axis=-1, keepdims=True)
            heads.append((_dot(p.astype(BF16), mvb_ref[h]) / l).astype(BF16))
        o = jnp.concatenate(heads, axis=-1)
        y_ref[0, rows, :] = _layer_norm(alpha * x + _dot(o, wo_ref[...]), g_ref[...], b_ref[...])


def _mem_attn(x, mk, mv, layer, wq, wo, g, b, alpha, tm, even=None):
    bn, tlen, _ = x.shape
    mlen = mk.shape[2]
    tok = lambda w: pl.BlockSpec((1, tm, w), lambda bb, t: (bb, t, 0))
    mspec = pl.BlockSpec((1, 1, mlen, MEM_HEADS, MEM_HEAD_DIM), lambda bb, t: (layer, bb, 0, 0, 0))
    pre_args, pre_specs = [], []
    if even is not None:
        att, gc, wmix, e, g0, b0 = even
        pre_args = [att, gc, wmix, g0, b0]
        pre_specs = [tok(FOX_DIM), tok(CONV_DIM), _resident(wmix.shape, e), _resident(g0.shape),
                     _resident(b0.shape)]
    return pl.pallas_call(
        functools.partial(_mem_attn_kernel, alpha=alpha, fuse_even=even is not None,
                          n_sub=max(1, tm // TOKEN_TILE)),
        grid=(bn, tlen // tm),
        in_specs=[tok(D_MODEL)] + pre_specs + [mspec, mspec, _resident(wq.shape, layer),
                                               _resident(wo.shape, layer), _resident(g.shape),
                                               _resident(b.shape)],
        out_specs=tok(D_MODEL),
        out_shape=jax.ShapeDtypeStruct((bn, tlen, D_MODEL), F32),
        scratch_shapes=[pltpu.VMEM((MEM_HEADS, mlen, MEM_HEAD_DIM), BF16),
                        pltpu.VMEM((MEM_HEADS, mlen, MEM_HEAD_DIM), BF16)],
        compiler_params=_params("parallel", "arbitrary"),
        name="mem_attn",
    )(x, *pre_args, mk, mv, wq, wo, g, b)


def _ffn_kernel(x_ref, wg_ref, wu_ref, wd_ref, g_ref, b_ref, y_ref, h_scr, *, alpha, n_sub):
    d_ff = wg_ref.shape[1]
    rows_per = x_ref.shape[0] // n_sub
    for s in range(n_sub):
        rows = slice(s * rows_per, (s + 1) * rows_per)
        x = x_ref[rows, :]
        xb = x.astype(BF16)
        for c in range(d_ff // FFN_CHUNK):
            cols = slice(c * FFN_CHUNK, (c + 1) * FFN_CHUNK)
            gate = _dot(xb, wg_ref[:, cols])
            up = _dot(xb, wu_ref[:, cols])
            h_scr[rows, cols] = (gate * jax.nn.sigmoid(gate) * up).astype(BF16)
        y = alpha * x + _dot(h_scr[rows, :], wd_ref[...])
        y_ref[rows, :] = _layer_norm(y, g_ref[...], b_ref[...])


def _ffn(x, layer, wg, wu, wd, g, b, alpha, tm):
    n = x.shape[0]
    d_ff = wg.shape[-1]
    assert d_ff % FFN_CHUNK == 0
    tok = pl.BlockSpec((tm, D_MODEL), lambda i: (i, 0))
    return pl.pallas_call(
        functools.partial(_ffn_kernel, alpha=alpha, n_sub=max(1, tm // TOKEN_TILE)),
        grid=(n // tm,),
        in_specs=[tok, _resident(wg.shape, layer), _resident(wu.shape, layer), _resident(wd.shape, layer),
                  _resident(g.shape), _resident(b.shape)],
        out_specs=tok,
        out_shape=jax.ShapeDtypeStruct((n, D_MODEL), F32),
        scratch_shapes=[pltpu.VMEM((tm, d_ff), BF16)],
        compiler_params=_params("parallel"),
        name="swiglu_ffn",
    )(x, wg, wu, wd, g, b)


def kernel(x_prompt, x_sample, cache_fox_k, cache_fox_v, cache_fox_logf, state_conv, cache_mem_k, cache_mem_v, mem_prompt, w_in_even, b_forget, conv_w, w_out_even, w_in_odd, b_in_odd, gmlp_norm_g, gmlp_norm_b, gmlp_w_s, gmlp_b_s, w_out_odd, mem_w_q, mem_w_k, mem_w_v, mem_w_o, ffn_w_gate, ffn_w_up, ffn_w_down, ln_g, ln_b):
    depth = ln_g.shape[0]
    alpha = float((2 * depth) ** 0.25)
    bp, s_len, d = x_prompt.shape
    bs, t_new, _ = x_sample.shape
    p_len = cache_fox_k.shape[2]
    row2 = lambda a: a.reshape(1, -1)

    w_out_even_b, w_in_odd_b, w_out_odd_b = (w.astype(BF16) for w in (w_out_even, w_in_odd, w_out_odd))
    mem_wq_b, mem_wo_b = mem_w_q.astype(BF16), mem_w_o.astype(BF16)
    ffn_wg_b, ffn_wu_b, ffn_wd_b = (w.astype(BF16) for w in (ffn_w_gate, ffn_w_up, ffn_w_down))

    mk_p, mv_p = _mem_kv(mem_prompt, mem_w_k.astype(BF16), mem_w_v.astype(BF16))

    yp, ys = x_prompt, x_sample
    fk_p, fv_p, fl_p, cs_p = [], [], [], []
    fk_s, fv_s, fl_s, cs_s, gv_s = [], [], [], [], []
    for layer in range(depth):
        lg, lb = ln_g[layer], ln_b[layer]
        even_p = even_s = None
        if layer % 2 == 0:
            e = layer // 2
            w_in = w_in_even[e].astype(BF16)
            qkv_w = w_in[:, 0:3 * FOX_DIM]
            f_wt = w_in[:, 3 * FOX_DIM:3 * FOX_DIM + FOX_HEADS].T
            c_w = w_in[:, 3 * FOX_DIM + FOX_HEADS:]
            f_b = b_forget[e].reshape(FOX_HEADS, 1)
            cw = conv_w[e]

            st0 = jnp.zeros((bp, SUBLANES, CONV_DIM), F32)
            q, kf, vf, kb, vb, lf, gc, cst, c = _even_inproj(yp, qkv_w, f_wt, c_w, f_b, cw, st0,
                                                             TOKEN_TILE, True)
            att = _fox_prompt(q, kb, vb, c, FOX_TILE)
            even_p = (att, gc, w_out_even_b, e, row2(lg[0]), row2(lb[0]))
            fk_p.append(kf.reshape(bp, s_len, FOX_HEADS, FOX_HEAD_DIM))
            fv_p.append(vf.reshape(bp, s_len, FOX_HEADS, FOX_HEAD_DIM))
            fl_p.append(jnp.swapaxes(lf, 1, 2))
            cs_p.append(cst)

            st = jnp.pad(state_conv[e], ((0, 0), (SUBLANES - (CONV_WIDTH - 1), 0), (0, 0)))
            q, kf, vf, kb, vb, lf, gc, cst = _even_inproj(ys, qkv_w, f_wt, c_w, f_b, cw, st, t_new, False)
            pad = -(p_len + t_new) % LANES
            lf_all = jnp.concatenate([jnp.swapaxes(cache_fox_logf[e], 1, 2), lf,
                                      jnp.zeros((bs, FOX_HEADS, pad), F32)], axis=2)
            att = _fox_sample(q, kb, vb, cache_fox_k[e].reshape(bs, p_len, FOX_DIM),
                              cache_fox_v[e].reshape(bs, p_len, FOX_DIM), lf_all)
            even_s = (att, gc, w_out_even_b, e, row2(lg[0]), row2(lb[0]))
            fk_s.append(kf.reshape(bs, t_new, FOX_HEADS, FOX_HEAD_DIM))
            fv_s.append(vf.reshape(bs, t_new, FOX_HEADS, FOX_HEAD_DIM))
            fl_s.append(jnp.swapaxes(lf, 1, 2))
            cs_s.append(cst)
        else:
            o = layer // 2
            args = (o, w_in_odd_b, row2(b_in_odd[o]), row2(gmlp_norm_g[o]), row2(gmlp_norm_b[o]),
                    gmlp_w_s, gmlp_b_s[o].T, w_out_odd_b, row2(lg[0]), row2(lb[0]), alpha)
            yp, _ = _gmlp_mixer(yp.reshape(-1, d), *args, TOKEN_TILE, False)
            yp = yp.reshape(bp, s_len, d)
            ys_pad = jnp.pad(ys, ((0, 0), (0, GMLP_CHUNK - t_new), (0, 0))).reshape(-1, d)
            ys_pad, vn = _gmlp_mixer(ys_pad, *args, TOKEN_TILE, True)
            ys = ys_pad.reshape(bs, GMLP_CHUNK, d)[:, :t_new]
            gv_s.append(vn.reshape(bs, GMLP_CHUNK, GMLP_DIM)[:, :t_new])

        yp = _mem_attn(yp, mk_p, mv_p, layer, mem_wq_b, mem_wo_b, row2(lg[1]), row2(lb[1]), alpha,
                       2 * TOKEN_TILE, even_p)
        ys = _mem_attn(ys, cache_mem_k, cache_mem_v, layer, mem_wq_b, mem_wo_b, row2(lg[1]), row2(lb[1]),
                       alpha, t_new, even_s)

        ffn_args = (layer, ffn_wg_b, ffn_wu_b, ffn_wd_b, row2(lg[2]), row2(lb[2]), alpha)
        yp = _ffn(yp.reshape(-1, d), *ffn_args, FFN_ROWS_PER_STEP).reshape(bp, s_len, d)
        ys = _ffn(ys.reshape(-1, d), *ffn_args, bs * t_new).reshape(bs, t_new, d)

    return (yp, ys, jnp.stack(fk_p), jnp.stack(fv_p), jnp.stack(fl_p), jnp.stack(cs_p), mk_p, mv_p,
            jnp.stack(fk_s), jnp.stack(fv_s), jnp.stack(fl_s), jnp.stack(cs_s), jnp.stack(gv_s))
```

```python
import functools

import jax
import jax.numpy as jnp
from jax import lax
from jax.experimental import pallas as pl
from jax.experimental.pallas import tpu as pltpu

F32 = jnp.float32
BF16 = jnp.bfloat16

D_MODEL = 1024
FOX_HEADS = 8
FOX_HEAD_DIM = 64
FOX_DIM = FOX_HEADS * FOX_HEAD_DIM
CONV_DIM = 512
CONV_WIDTH = 3
GMLP_CHUNK = 128
GMLP_GROUPS = 8
GMLP_DIM = D_MODEL
MEM_HEADS = 4
MEM_HEAD_DIM = 128
MEM_DIM = MEM_HEADS * MEM_HEAD_DIM
LN_EPS = 1e-5

LANES = 128
SUBLANES = 8
VMEM_LIMIT_BYTES = 56 * 1024 * 1024
TOKEN_TILE = 512
FOX_TILE = 512
FOX_KEY_SPLIT = 2
FFN_CHUNK = 256
FFN_ROWS_PER_STEP = 2 * TOKEN_TILE
LOG2E = 1.4426950408889634
CAST_BLOCK_BYTES = 4 * 1024 * 1024


def _params(*semantics):
    return pltpu.CompilerParams(dimension_semantics=semantics,
                                vmem_limit_bytes=VMEM_LIMIT_BYTES)


def _resident(shape, layer=None):
    if layer is None:
        nd = len(shape)
        return pl.BlockSpec(shape, lambda *_: (0,) * nd, pipeline_mode=pl.Buffered(1))
    nd = len(shape) - 1
    return pl.BlockSpec((None,) + tuple(shape[1:]), lambda *_: (layer,) + (0,) * nd,
                        pipeline_mode=pl.Buffered(1))


def _dot(a, b):
    return jnp.dot(a, b, preferred_element_type=F32)


def _dot_nt(a, b):
    return lax.dot_general(a, b, (((1,), (1,)), ((), ())), preferred_element_type=F32)


def _layer_norm(y, g, b):
    mu = jnp.mean(y, axis=-1, keepdims=True)
    d = y - mu
    var = jnp.mean(d * d, axis=-1, keepdims=True)
    return d * lax.rsqrt(var + LN_EPS) * g + b


def _gelu(z):
    return 0.5 * z * (1.0 + lax.erf(z * (2.0 ** -0.5)))


def _cast_kernel(x_ref, o_ref):
    o_ref[...] = x_ref[...].astype(o_ref.dtype)


def _to_bf16(w):
    n = w.shape[-1]
    rows = w.size // n
    blk = rows
    while blk * n * w.dtype.itemsize > CAST_BLOCK_BYTES and blk % (4 * SUBLANES) == 0:
        blk //= 2
    spec = pl.BlockSpec((blk, n), lambda i: (i, 0))
    out = pl.pallas_call(
        _cast_kernel, grid=(rows // blk,), in_specs=[spec], out_specs=spec,
        out_shape=jax.ShapeDtypeStruct((rows, n), BF16),
        compiler_params=_params("parallel"), name="to_bf16",
    )(w.reshape(rows, n))
    return out.reshape(w.shape)


def _lane_prefix_sum(x, carry):
    rows, n = x.shape
    lane = lax.broadcasted_iota(jnp.int32, (rows, LANES), 1)
    segs = []
    for j in range(n // LANES):
        seg = x[:, j * LANES:(j + 1) * LANES]
        step = 1
        while step < LANES:
            seg = seg + jnp.where(lane >= step, pltpu.roll(seg, step, 1), 0.0)
            step *= 2
        segs.append(seg)
    outs = []
    for seg in segs:
        seg = seg + carry
        outs.append(seg)
        carry = seg[:, LANES - 1:LANES]
    return jnp.concatenate(outs, axis=1), carry


def _even_inproj_kernel(x_ref, wqkv_ref, wft_ref, wc_ref, bf_ref, cw_ref, st_ref,
                        q_ref, kf_ref, vf_ref, kb_ref, vb_ref, lf_ref, gc_ref, cs_ref, *rest,
                        tm, nt, emit_c):
    if emit_c:
        c_ref, prev_ref, ccarry_ref = rest
    else:
        (prev_ref,) = rest
    t = pl.program_id(1)

    @pl.when(t == 0)
    def _():
        prev_ref[...] = st_ref[0]
        if emit_c:
            ccarry_ref[...] = jnp.zeros_like(ccarry_ref)

    xb = x_ref[0].astype(BF16)
    lf = jax.nn.log_sigmoid(_dot_nt(wft_ref[...], xb) + bf_ref[...])
    lf_ref[0] = lf
    if emit_c:
        c, carry = _lane_prefix_sum(lf, ccarry_ref[...])
        c_ref[0] = c
        ccarry_ref[...] = carry

    h = _dot(xb, wc_ref[:, 0:CONV_DIM])
    bg = _dot(xb, wc_ref[:, CONV_DIM:2 * CONV_DIM])
    cg = _dot(xb, wc_ref[:, 2 * CONV_DIM:3 * CONV_DIM])
    pre = cg * h

    prev = prev_ref[...]
    row = lax.broadcasted_iota(jnp.int32, (tm, CONV_DIM), 0)
    p1 = jnp.where(row == 0, prev[7:8, :], pltpu.roll(pre, 1, 0))
    p2 = jnp.where(row == 0, prev[6:7, :],
                   jnp.where(row == 1, prev[7:8, :], pltpu.roll(pre, 2, 0)))
    cw = cw_ref[...]
    conv = cw[0:1, :] * p2 + cw[1:2, :] * p1 + cw[2:3, :] * pre
    gc_ref[0] = (bg * conv).astype(BF16)
    prev_ref[...] = pre[tm - SUBLANES:tm, :]

    q = _dot(xb, wqkv_ref[:, 0:FOX_DIM])
    q_ref[0] = (q * (LOG2E * FOX_HEAD_DIM ** -0.5)).astype(BF16)
    k = _dot(xb, wqkv_ref[:, FOX_DIM:2 * FOX_DIM])
    kf_ref[0] = k
    kb_ref[0] = k.astype(BF16)
    v = _dot(xb, wqkv_ref[:, 2 * FOX_DIM:3 * FOX_DIM])
    vf_ref[0] = v
    vb_ref[0] = v.astype(BF16)

    @pl.when(t == nt - 1)
    def _():
        cs_ref[0] = pre[tm - (CONV_WIDTH - 1):tm, :]


def _even_inproj(x, wqkv, wft, wc, bf, cw, state8, tm, emit_c):
    bn, tlen, _ = x.shape
    nt = tlen // tm
    tok = lambda w: pl.BlockSpec((1, tm, w), lambda b, t: (b, t, 0))
    tlane = pl.BlockSpec((1, FOX_HEADS, tm), lambda b, t: (b, 0, t))
    out_shape = [
        jax.ShapeDtypeStruct((bn, tlen, FOX_DIM), BF16),
        jax.ShapeDtypeStruct((bn, tlen, FOX_DIM), F32),
        jax.ShapeDtypeStruct((bn, tlen, FOX_DIM), F32),
        jax.ShapeDtypeStruct((bn, tlen, FOX_DIM), BF16),
        jax.ShapeDtypeStruct((bn, tlen, FOX_DIM), BF16),
        jax.ShapeDtypeStruct((bn, FOX_HEADS, tlen), F32),
        jax.ShapeDtypeStruct((bn, tlen, CONV_DIM), BF16),
        jax.ShapeDtypeStruct((bn, CONV_WIDTH - 1, CONV_DIM), F32),
    ]
    out_specs = [tok(FOX_DIM), tok(FOX_DIM), tok(FOX_DIM), tok(FOX_DIM), tok(FOX_DIM), tlane,
                 tok(CONV_DIM), pl.BlockSpec((1, CONV_WIDTH - 1, CONV_DIM), lambda b, t: (b, 0, 0))]
    scratch = [pltpu.VMEM((SUBLANES, CONV_DIM), F32)]
    if emit_c:
        out_shape.append(jax.ShapeDtypeStruct((bn, FOX_HEADS, tlen), F32))
        out_specs.append(tlane)
        scratch.append(pltpu.VMEM((FOX_HEADS, 1), F32))
    return pl.pallas_call(
        functools.partial(_even_inproj_kernel, tm=tm, nt=nt, emit_c=emit_c),
        grid=(bn, nt),
        in_specs=[tok(D_MODEL), _resident(wqkv.shape), _resident(wft.shape),
                  _resident(wc.shape), _resident(bf.shape), _resident(cw.shape),
                  pl.BlockSpec((1, SUBLANES, CONV_DIM), lambda b, t: (b, 0, 0))],
        out_specs=tuple(out_specs),
        out_shape=tuple(out_shape),
        scratch_shapes=scratch,
        compiler_params=_params("parallel", "arbitrary"),
        name="even_inproj",
    )(x, wqkv, wft, wc, bf, cw, state8)


def _head_mask(shape, hh):
    lane = lax.broadcasted_iota(jnp.int32, shape, 1)
    return (lane >= FOX_HEAD_DIM) if hh == 1 else (lane < FOX_HEAD_DIM)


def _fox_prompt_kernel(q_ref, k_ref, v_ref, c_ref, o_ref, vt_ref, cb_ref, *, tq, nq):
    hp = pl.program_id(1)
    i = pl.program_id(2)
    half = lax.broadcasted_iota(jnp.int32, (LANES, tq), 0) < FOX_HEAD_DIM

    @pl.when(i == 0)
    def _():
        for j in range(nq):
            rows = slice(j * tq, (j + 1) * tq)
            vt = v_ref[0, rows, :].astype(F32).T
            for hh in range(2):
                own = half if hh == 0 else jnp.logical_not(half)
                vt_ref[hh, j] = jnp.where(own, vt, 1.0).astype(BF16)
                c_j = c_ref[0, (hp * 2 + hh) * nq + j]
                rel = (c_j[:, 0:1] - c_j) * LOG2E
                cb_ref[hh, rows, :] = jnp.broadcast_to(rel, (LANES, tq)).T

    q = q_ref[0]
    qh = [jnp.where(_head_mask(q.shape, hh), q, jnp.zeros_like(q)) for hh in range(2)]
    wk = tq // FOX_KEY_SPLIT
    key = lax.broadcasted_iota(jnp.int32, (wk, tq), 0)
    qry = lax.broadcasted_iota(jnp.int32, (wk, tq), 1)

    def c_start(j, hh):
        return c_ref[0, (hp * 2 + hh) * nq + j][:, 0:1]

    def scores(j, kb, hh):
        rows = pl.ds(pl.multiple_of(j * tq + kb * wk, wk), wk)
        st = _dot_nt(k_ref[0, rows, :], qh[hh])
        return st + jnp.tile(cb_ref[hh, rows, :], (1, tq // LANES))

    def chain(tiles, carry, diagonal=False):
        blocks = [(i, kb, True) for kb in range(FOX_KEY_SPLIT)] if diagonal else []
        blocks += [(j, kb, False) for j in tiles for kb in range(FOX_KEY_SPLIT)]
        sts = [[scores(j, kb, hh) for hh in range(2)] for j, kb, _ in blocks]
        carry = [None, None] if carry is None else list(carry)
        for (j, kb, diag), st_b in zip(blocks, sts):
            for hh in range(2):
                st = st_b[hh]
                if diag:
                    st = jnp.where(key + kb * wk <= qry, st, -jnp.inf)
                    top = jnp.max(st, axis=0, keepdims=True)
                else:
                    delta = (c_start(i, hh) - c_start(j, hh)) * LOG2E
                    top = jnp.max(st, axis=0, keepdims=True) + delta
                m_new = top if carry[hh] is None else jnp.maximum(carry[hh][0], top)
                p = jnp.exp2(st - (m_new if diag else m_new - delta)).astype(BF16)
                pv = _dot(vt_ref[hh, j, :, kb * wk:(kb + 1) * wk], p)
                if carry[hh] is None:
                    carry[hh] = (m_new, pv)
                else:
                    carry[hh] = (m_new, jnp.exp2(carry[hh][0] - m_new) * carry[hh][1] + pv)
        return tuple(carry)

    carry = lax.cond(i % 2 == 1, lambda: chain([i - 1], None, True), lambda: chain([], None, True))
    carry = lax.fori_loop(0, i // 2, lambda jj, c: chain([2 * jj, 2 * jj + 1], c), carry)
    out0 = carry[0][1] / carry[0][1][FOX_HEAD_DIM:FOX_HEAD_DIM + 1, :]
    out1 = carry[1][1] / carry[1][1][0:1, :]
    o_ref[0] = jnp.where(half, out0, out1).T.astype(o_ref.dtype)


def _fox_prompt(q, k, v, c, tq):
    bn, s_len, _ = q.shape
    nq = s_len // tq
    c4 = c.reshape(bn, FOX_HEADS * nq, 1, tq)
    qspec = pl.BlockSpec((1, tq, LANES), lambda b, h, i: (b, i, h))
    kspec = pl.BlockSpec((1, s_len, LANES), lambda b, h, i: (b, 0, h))
    return pl.pallas_call(
        functools.partial(_fox_prompt_kernel, tq=tq, nq=nq),
        grid=(bn, FOX_DIM // LANES, nq),
        in_specs=[qspec, kspec, kspec,
                  pl.BlockSpec((1, FOX_HEADS * nq, 1, tq), lambda b, h, i: (b, 0, 0, 0))],
        out_specs=qspec,
        out_shape=jax.ShapeDtypeStruct((bn, s_len, FOX_DIM), BF16),
        scratch_shapes=[pltpu.VMEM((2, nq, LANES, tq), BF16),
                        pltpu.VMEM((2, s_len, LANES), F32)],
        compiler_params=_params("parallel", "parallel", "arbitrary"),
        name="fox_prompt",
    )(q, k, v, c4)


def _fox_sample_kernel(q_ref, kn_ref, vn_ref, kc_ref, vc_ref, lf_ref, o_ref, *, t_new, p_len):
    n_new = lf_ref.shape[-1] - p_len
    zeros = jnp.zeros((n_new - t_new, LANES), BF16)
    row = lax.broadcasted_iota(jnp.int32, (t_new, n_new), 0)
    col = lax.broadcasted_iota(jnp.int32, (t_new, n_new), 1)
    c_all, _ = _lane_prefix_sum(lf_ref[0], jnp.zeros((FOX_HEADS, 1), F32))
    for hp in range(FOX_DIM // LANES):
        lanes = slice(hp * LANES, (hp + 1) * LANES)
        q = q_ref[0, :, lanes]
        kn = jnp.concatenate([kn_ref[0, :, lanes], zeros], axis=0)
        vn = jnp.concatenate([vn_ref[0, :, lanes], zeros], axis=0)
        kc = kc_ref[0, :, lanes].astype(BF16)
        vc = vc_ref[0, :, lanes].astype(BF16)
        outs = []
        for hh in range(2):
            head = hp * 2 + hh
            qh = jnp.where(_head_mask(q.shape, hh), q, jnp.zeros_like(q))
            c_row = c_all[head:head + 1, :]
            bias = (c_row[:, p_len - 1:p_len] - c_row) * LOG2E
            s_c = _dot_nt(qh, kc) + bias[:, 0:p_len]
            s_n = jnp.where(col <= row, _dot_nt(qh, kn) + bias[:, p_len:], -jnp.inf)
            m = jnp.maximum(jnp.max(s_c, axis=-1, keepdims=True), jnp.max(s_n, axis=-1, keepdims=True))
            p_c = jnp.exp2(s_c - m)
            p_n = jnp.exp2(s_n - m)
            l = jnp.sum(p_c, axis=-1, keepdims=True) + jnp.sum(p_n, axis=-1, keepdims=True)
            outs.append((_dot(p_c.astype(BF16), vc) + _dot(p_n.astype(BF16), vn)) / l)
        o_ref[0, :, lanes] = jnp.where(_head_mask(outs[0].shape, 0), outs[0], outs[1]).astype(o_ref.dtype)


def _fox_sample(q, kn, vn, kc, vc, lf_all):
    bn, t_new, _ = q.shape
    p_len = kc.shape[1]
    n_keys = lf_all.shape[-1]
    qspec = pl.BlockSpec((1, t_new, FOX_DIM), lambda b: (b, 0, 0))
    cspec = pl.BlockSpec((1, p_len, FOX_DIM), lambda b: (b, 0, 0))
    return pl.pallas_call(
        functools.partial(_fox_sample_kernel, t_new=t_new, p_len=p_len),
        grid=(bn,),
        in_specs=[qspec, qspec, qspec, cspec, cspec,
                  pl.BlockSpec((1, FOX_HEADS, n_keys), lambda b: (b, 0, 0))],
        out_specs=qspec,
        out_shape=jax.ShapeDtypeStruct((bn, t_new, FOX_DIM), BF16),
        compiler_params=_params("parallel"),
        name="fox_sample",
    )(q, kn, vn, kc, vc, lf_all)


def _gmlp_kernel(x_ref, win_ref, bin_ref, ng_ref, nb_ref, ws_ref, bst_ref, wout_ref,
                 g_ref, b_ref, y_ref, *rest, tm, alpha, emit_vn):
    if emit_vn:
        vn_ref, u_scr, vnb_scr, gated_scr = rest
    else:
        u_scr, vnb_scr, gated_scr = rest
    x = x_ref[...]
    xb = x.astype(BF16)
    v = _gelu(_dot(xb, win_ref[:, GMLP_DIM:]) + bin_ref[:, GMLP_DIM:])
    vn = _layer_norm(v, ng_ref[...], nb_ref[...])
    if emit_vn:
        vn_ref[...] = vn
    vnb_scr[...] = vn.astype(BF16)
    u_scr[...] = _gelu(_dot(xb, win_ref[:, 0:GMLP_DIM]) + bin_ref[:, 0:GMLP_DIM])

    row = lax.broadcasted_iota(jnp.int32, (GMLP_CHUNK, GMLP_CHUNK), 0)
    col = lax.broadcasted_iota(jnp.int32, (GMLP_CHUNK, GMLP_CHUNK), 1)
    gd = GMLP_DIM // GMLP_GROUPS
    n_chunks = tm // GMLP_CHUNK
    for g in range(GMLP_GROUPS):
        wm = jnp.where(col <= row, ws_ref[g], 0.0).astype(BF16)
        bias = bst_ref[:, g:g + 1]
        cs = slice(g * gd, (g + 1) * gd)
        for r in range(0, n_chunks, 2):
            rss = [slice(rr * GMLP_CHUNK, (rr + 1) * GMLP_CHUNK) for rr in range(r, min(r + 2, n_chunks))]
            mixed = _dot(wm, jnp.concatenate([vnb_scr[rs, cs] for rs in rss], axis=1)) + bias
            for n, rs in enumerate(rss):
                gated_scr[rs, cs] = (u_scr[rs, cs] * mixed[:, n * gd:(n + 1) * gd]).astype(BF16)
    out = _dot(gated_scr[...], wout_ref[...])
    y_ref[...] = _layer_norm(alpha * x + out, g_ref[...], b_ref[...])


def _gmlp_mixer(x, layer, win, bin_, ng, nb, ws, bst, wout, g, b, alpha, tm, emit_vn):
    n = x.shape[0]
    tok = pl.BlockSpec((tm, D_MODEL), lambda i: (i, 0))
    out_shape = [jax.ShapeDtypeStruct((n, D_MODEL), F32)]
    out_specs = [tok]
    if emit_vn:
        out_shape.append(jax.ShapeDtypeStruct((n, GMLP_DIM), F32))
        out_specs.append(tok)
    res = pl.pallas_call(
        functools.partial(_gmlp_kernel, tm=tm, alpha=alpha, emit_vn=emit_vn),
        grid=(n // tm,),
        in_specs=[tok, _resident(win.shape, layer), _resident(bin_.shape), _resident(ng.shape),
                  _resident(nb.shape), _resident(ws.shape, layer), _resident(bst.shape),
                  _resident(wout.shape, layer), _resident(g.shape), _resident(b.shape)],
        out_specs=tuple(out_specs),
        out_shape=tuple(out_shape),
        scratch_shapes=[pltpu.VMEM((tm, GMLP_DIM), F32), pltpu.VMEM((tm, GMLP_DIM), BF16),
                        pltpu.VMEM((tm, GMLP_DIM), BF16)],
        compiler_params=_params("parallel"),
        name="gmlp_mixer",
    )(x, win, bin_, ng, nb, ws, bst, wout, g, b)
    return res if emit_vn else (res[0], None)


def _mem_attn_kernel(x_ref, *refs, alpha, fuse_even, project_kv, n_sub):
    if fuse_even:
        att_ref, gc_ref, wmix_ref, g0_ref, b0_ref = refs[:5]
        refs = refs[5:]
    if project_kv:
        (mem_ref, wk_ref, wv_ref, wq_ref, wo_ref, g_ref, b_ref,
         y_ref, mk_ref, mv_ref, mkb_ref, mvb_ref) = refs
    else:
        mk_ref, mv_ref, wq_ref, wo_ref, g_ref, b_ref, y_ref, mkb_ref, mvb_ref = refs

    @pl.when(pl.program_id(1) == 0)
    def _():
        if project_kv:
            mb = mem_ref[0].astype(BF16)
            k = _dot(mb, wk_ref[...])
            v = _dot(mb, wv_ref[...])
        for h in range(MEM_HEADS):
            hs = slice(h * MEM_HEAD_DIM, (h + 1) * MEM_HEAD_DIM)
            if project_kv:
                mk_ref[0, :, h, :] = k[:, hs]
                mv_ref[0, :, h, :] = v[:, hs]
                mkb_ref[h] = k[:, hs].astype(BF16)
                mvb_ref[h] = v[:, hs].astype(BF16)
            else:
                mkb_ref[h] = mk_ref[0, 0, :, h, :].astype(BF16)
                mvb_ref[h] = mv_ref[0, 0, :, h, :].astype(BF16)

    rows_per = x_ref.shape[1] // n_sub
    for sb in range(n_sub):
        rows = slice(sb * rows_per, (sb + 1) * rows_per)
        x = x_ref[0, rows, :]
        if fuse_even:
            mix = _dot(att_ref[0, rows, :], wmix_ref[0:FOX_DIM, :]) + _dot(gc_ref[0, rows, :], wmix_ref[FOX_DIM:, :])
            x = _layer_norm(alpha * x + mix, g0_ref[...], b0_ref[...])
        q = _dot(x.astype(BF16), wq_ref[...]) * (LOG2E * MEM_HEAD_DIM ** -0.5)
        qb = q.astype(BF16)
        heads = []
        for h in range(MEM_HEADS):
            hs = slice(h * MEM_HEAD_DIM, (h + 1) * MEM_HEAD_DIM)
            s = _dot_nt(qb[:, hs], mkb_ref[h])
            m = jnp.max(s, axis=-1, keepdims=True)
            p = jnp.exp2(s - m)
            l = jnp.sum(p, axis=-1, keepdims=True)
            heads.append((_dot(p.astype(BF16), mvb_ref[h]) / l).astype(BF16))
        o = jnp.concatenate(heads, axis=-1)
        y_ref[0, rows, :] = _layer_norm(alpha * x + _dot(o, wo_ref[...]), g_ref[...], b_ref[...])


def _mem_attn(x, kv, layer, wq, wo, g, b, alpha, tm, even=None):
    bn, tlen, _ = x.shape
    project_kv = len(kv) == 3
    mlen = kv[0].shape[1] if project_kv else kv[0].shape[2]
    tok = lambda w: pl.BlockSpec((1, tm, w), lambda bb, t: (bb, t, 0))
    if project_kv:
        kv_specs = [pl.BlockSpec((1, mlen, D_MODEL), lambda bb, t: (bb, 0, 0)),
                    _resident(kv[1].shape, layer), _resident(kv[2].shape, layer)]
    else:
        kv_specs = [pl.BlockSpec((1, 1, mlen, MEM_HEADS, MEM_HEAD_DIM), lambda bb, t: (layer, bb, 0, 0, 0))] * 2
    pre_args, pre_specs = [], []
    if even is not None:
        att, gc, wmix, e, g0, b0 = even
        pre_args = [att, gc, wmix, g0, b0]
        pre_specs = [tok(FOX_DIM), tok(CONV_DIM), _resident(wmix.shape, e), _resident(g0.shape),
                     _resident(b0.shape)]
    out_specs = [tok(D_MODEL)]
    out_shape = [jax.ShapeDtypeStruct((bn, tlen, D_MODEL), F32)]
    if project_kv:
        out_specs += [pl.BlockSpec((1, mlen, MEM_HEADS, MEM_HEAD_DIM), lambda bb, t: (bb, 0, 0, 0))] * 2
        out_shape += [jax.ShapeDtypeStruct((bn, mlen, MEM_HEADS, MEM_HEAD_DIM), F32)] * 2
    res = pl.pallas_call(
        functools.partial(_mem_attn_kernel, alpha=alpha, fuse_even=even is not None,
                          project_kv=project_kv, n_sub=max(1, tm // TOKEN_TILE)),
        grid=(bn, tlen // tm),
        in_specs=[tok(D_MODEL)] + pre_specs + kv_specs + [_resident(wq.shape, layer), _resident(wo.shape, layer),
                                                          _resident(g.shape), _resident(b.shape)],
        out_specs=tuple(out_specs),
        out_shape=tuple(out_shape),
        scratch_shapes=[pltpu.VMEM((MEM_HEADS, mlen, MEM_HEAD_DIM), BF16),
                        pltpu.VMEM((MEM_HEADS, mlen, MEM_HEAD_DIM), BF16)],
        compiler_params=_params("parallel", "arbitrary"),
        name="mem_attn",
    )(x, *pre_args, *kv, wq, wo, g, b)
    return res if project_kv else res[0]


def _ffn_kernel(x_ref, wg_ref, wu_ref, wd_ref, g_ref, b_ref, y_ref, h_scr, *, alpha, n_sub):
    d_ff = wg_ref.shape[1]
    rows_per = x_ref.shape[0] // n_sub
    for s in range(n_sub):
        rows = slice(s * rows_per, (s + 1) * rows_per)
        x = x_ref[rows, :]
        xb = x.astype(BF16)
        for c in range(d_ff // FFN_CHUNK):
            cols = slice(c * FFN_CHUNK, (c + 1) * FFN_CHUNK)
            gate = _dot(xb, wg_ref[:, cols])
            up = _dot(xb, wu_ref[:, cols])
            h_scr[rows, cols] = (gate * jax.nn.sigmoid(gate) * up).astype(BF16)
        y = alpha * x + _dot(h_scr[rows, :], wd_ref[...])
        y_ref[rows, :] = _layer_norm(y, g_ref[...], b_ref[...])


def _ffn(x, layer, wg, wu, wd, g, b, alpha, tm):
    n = x.shape[0]
    d_ff = wg.shape[-1]
    assert d_ff % FFN_CHUNK == 0
    tok = pl.BlockSpec((tm, D_MODEL), lambda i: (i, 0))
    return pl.pallas_call(
        functools.partial(_ffn_kernel, alpha=alpha, n_sub=max(1, tm // TOKEN_TILE)),
        grid=(n // tm,),
        in_specs=[tok, _resident(wg.shape, layer), _resident(wu.shape, layer), _resident(wd.shape, layer),
                  _resident(g.shape), _resident(b.shape)],
        out_specs=tok,
        out_shape=jax.ShapeDtypeStruct((n, D_MODEL), F32),
        scratch_shapes=[pltpu.VMEM((tm, d_ff), BF16)],
        compiler_params=_params("parallel"),
        name="swiglu_ffn",
    )(x, wg, wu, wd, g, b)


def kernel(x_prompt, x_sample, cache_fox_k, cache_fox_v, cache_fox_logf, state_conv, cache_mem_k, cache_mem_v, mem_prompt, w_in_even, b_forget, conv_w, w_out_even, w_in_odd, b_in_odd, gmlp_norm_g, gmlp_norm_b, gmlp_w_s, gmlp_b_s, w_out_odd, mem_w_q, mem_w_k, mem_w_v, mem_w_o, ffn_w_gate, ffn_w_up, ffn_w_down, ln_g, ln_b):
    depth = ln_g.shape[0]
    alpha = float((2 * depth) ** 0.25)
    bp, s_len, d = x_prompt.shape
    bs, t_new, _ = x_sample.shape
    p_len = cache_fox_k.shape[2]
    row2 = lambda a: a.reshape(1, -1)

    w_out_even_b, w_in_odd_b, w_out_odd_b = (_to_bf16(w) for w in (w_out_even, w_in_odd, w_out_odd))
    mem_wq_b, mem_wo_b, mem_wk_b, mem_wv_b = (_to_bf16(w) for w in (mem_w_q, mem_w_o, mem_w_k, mem_w_v))
    ffn_wg_b, ffn_wu_b, ffn_wd_b = (_to_bf16(w) for w in (ffn_w_gate, ffn_w_up, ffn_w_down))

    yp, ys = x_prompt, x_sample
    fk_p, fv_p, fl_p, cs_p, mk_p, mv_p = [], [], [], [], [], []
    fk_s, fv_s, fl_s, cs_s, gv_s = [], [], [], [], []
    for layer in range(depth):
        lg, lb = ln_g[layer], ln_b[layer]
        even_p = even_s = None
        if layer % 2 == 0:
            e = layer // 2
            w_in = w_in_even[e].astype(BF16)
            qkv_w = w_in[:, 0:3 * FOX_DIM]
            f_wt = w_in[:, 3 * FOX_DIM:3 * FOX_DIM + FOX_HEADS].T
            c_w = w_in[:, 3 * FOX_DIM + FOX_HEADS:]
            f_b = b_forget[e].reshape(FOX_HEADS, 1)
            cw = conv_w[e]

            st0 = jnp.zeros((bp, SUBLANES, CONV_DIM), F32)
            q, kf, vf, kb, vb, lf, gc, cst, c = _even_inproj(yp, qkv_w, f_wt, c_w, f_b, cw, st0,
                                                             TOKEN_TILE, True)
            att = _fox_prompt(q, kb, vb, c, FOX_TILE)
            even_p = (att, gc, w_out_even_b, e, row2(lg[0]), row2(lb[0]))
            fk_p.append(kf.reshape(bp, s_len, FOX_HEADS, FOX_HEAD_DIM))
            fv_p.append(vf.reshape(bp, s_len, FOX_HEADS, FOX_HEAD_DIM))
            fl_p.append(jnp.swapaxes(lf, 1, 2))
            cs_p.append(cst)

            st = jnp.pad(state_conv[e], ((0, 0), (SUBLANES - (CONV_WIDTH - 1), 0), (0, 0)))
            q, kf, vf, kb, vb, lf, gc, cst = _even_inproj(ys, qkv_w, f_wt, c_w, f_b, cw, st, t_new, False)
            pad = -(p_len + t_new) % LANES
            lf_all = jnp.concatenate([jnp.swapaxes(cache_fox_logf[e], 1, 2), lf,
                                      jnp.zeros((bs, FOX_HEADS, pad), F32)], axis=2)
            att = _fox_sample(q, kb, vb, cache_fox_k[e].reshape(bs, p_len, FOX_DIM),
                              cache_fox_v[e].reshape(bs, p_len, FOX_DIM), lf_all)
            even_s = (att, gc, w_out_even_b, e, row2(lg[0]), row2(lb[0]))
            fk_s.append(kf.reshape(bs, t_new, FOX_HEADS, FOX_HEAD_DIM))
            fv_s.append(vf.reshape(bs, t_new, FOX_HEADS, FOX_HEAD_DIM))
            fl_s.append(jnp.swapaxes(lf, 1, 2))
            cs_s.append(cst)
        else:
            o = layer // 2
            args = (o, w_in_odd_b, row2(b_in_odd[o]), row2(gmlp_norm_g[o]), row2(gmlp_norm_b[o]),
                    gmlp_w_s, gmlp_b_s[o].T, w_out_odd_b, row2(lg[0]), row2(lb[0]), alpha)
            yp, _ = _gmlp_mixer(yp.reshape(-1, d), *args, TOKEN_TILE, False)
            yp = yp.reshape(bp, s_len, d)
            ys_pad = jnp.pad(ys, ((0, 0), (0, GMLP_CHUNK - t_new), (0, 0))).reshape(-1, d)
            ys_pad, vn = _gmlp_mixer(ys_pad, *args, TOKEN_TILE, True)
            ys = ys_pad.reshape(bs, GMLP_CHUNK, d)[:, :t_new]
            gv_s.append(vn.reshape(bs, GMLP_CHUNK, GMLP_DIM)[:, :t_new])

        yp, mk, mv = _mem_attn(yp, (mem_prompt, mem_wk_b, mem_wv_b), layer, mem_wq_b, mem_wo_b,
                               row2(lg[1]), row2(lb[1]), alpha, 2 * TOKEN_TILE, even_p)
        mk_p.append(mk)
        mv_p.append(mv)
        ys = _mem_attn(ys, (cache_mem_k, cache_mem_v), layer, mem_wq_b, mem_wo_b, row2(lg[1]), row2(lb[1]),
                       alpha, t_new, even_s)

        ffn_args = (layer, ffn_wg_b, ffn_wu_b, ffn_wd_b, row2(lg[2]), row2(lb[2]), alpha)
        yp = _ffn(yp.reshape(-1, d), *ffn_args, FFN_ROWS_PER_STEP).reshape(bp, s_len, d)
        ys = _ffn(ys.reshape(-1, d), *ffn_args, bs * t_new).reshape(bs, t_new, d)

    return (yp, ys, jnp.stack(fk_p), jnp.stack(fv_p), jnp.stack(fl_p), jnp.stack(cs_p),
            jnp.stack(mk_p), jnp.stack(mv_p),
            jnp.stack(fk_s), jnp.stack(fv_s), jnp.stack(fl_s), jnp.stack(cs_s), jnp.stack(gv_s))
```

```python
import functools

import jax
import jax.numpy as jnp
from jax import lax
from jax.experimental import pallas as pl
from jax.experimental.pallas import tpu as pltpu

F32 = jnp.float32
BF16 = jnp.bfloat16

D_MODEL = 1024
FOX_HEADS = 8
FOX_HEAD_DIM = 64
FOX_DIM = FOX_HEADS * FOX_HEAD_DIM
CONV_DIM = 512
CONV_WIDTH = 3
GMLP_CHUNK = 128
GMLP_GROUPS = 8
GMLP_DIM = D_MODEL
MEM_HEADS = 4
MEM_HEAD_DIM = 128
MEM_DIM = MEM_HEADS * MEM_HEAD_DIM
LN_EPS = 1e-5

LANES = 128
SUBLANES = 8
VMEM_LIMIT_BYTES = 56 * 1024 * 1024
TOKEN_TILE = 512
FOX_TILE = 512
FOX_KEY_SPLIT = 2
FFN_CHUNK = 256
FFN_ROWS_PER_STEP = 2 * TOKEN_TILE
LOG2E = 1.4426950408889634
CAST_BLOCK_BYTES = 1024 * 1024


def _params(*semantics):
    return pltpu.CompilerParams(dimension_semantics=semantics,
                                vmem_limit_bytes=VMEM_LIMIT_BYTES)


def _resident(shape, layer=None):
    if layer is None:
        nd = len(shape)
        return pl.BlockSpec(shape, lambda *_: (0,) * nd, pipeline_mode=pl.Buffered(1))
    nd = len(shape) - 1
    return pl.BlockSpec((None,) + tuple(shape[1:]), lambda *_: (layer,) + (0,) * nd,
                        pipeline_mode=pl.Buffered(1))


def _dot(a, b):
    return jnp.dot(a, b, preferred_element_type=F32)


def _dot_nt(a, b):
    return lax.dot_general(a, b, (((1,), (1,)), ((), ())), preferred_element_type=F32)


def _layer_norm(y, g, b):
    mu = jnp.mean(y, axis=-1, keepdims=True)
    d = y - mu
    var = jnp.mean(d * d, axis=-1, keepdims=True)
    return d * lax.rsqrt(var + LN_EPS) * g + b


def _gelu(z):
    return 0.5 * z * (1.0 + lax.erf(z * (2.0 ** -0.5)))


def _cast_kernel(*refs, n_blocks):
    n = len(n_blocks)
    i = pl.program_id(0)
    for k in range(n):
        @pl.when(i < n_blocks[k])
        def _(k=k):
            refs[n + k][...] = refs[k][...].astype(BF16)


def _to_bf16(ws):
    flat, specs, n_blocks = [], [], []
    for w in ws:
        n = w.shape[-1]
        rows = w.size // n
        blk = rows
        while blk * n * w.dtype.itemsize > CAST_BLOCK_BYTES and blk % (4 * SUBLANES) == 0:
            blk //= 2
        nb = rows // blk
        flat.append(w.reshape(rows, n))
        specs.append(pl.BlockSpec((blk, n), lambda i, nb=nb: (jnp.minimum(i, nb - 1), 0)))
        n_blocks.append(nb)
    outs = pl.pallas_call(
        functools.partial(_cast_kernel, n_blocks=tuple(n_blocks)),
        grid=(max(n_blocks),), in_specs=specs, out_specs=tuple(specs),
        out_shape=tuple(jax.ShapeDtypeStruct(f.shape, BF16) for f in flat),
        compiler_params=_params("arbitrary"), name="to_bf16",
    )(*flat)
    return [o.reshape(w.shape) for o, w in zip(outs, ws)]


def _lane_prefix_sum(x, carry):
    rows, n = x.shape
    lane = lax.broadcasted_iota(jnp.int32, (rows, LANES), 1)
    segs = []
    for j in range(n // LANES):
        seg = x[:, j * LANES:(j + 1) * LANES]
        step = 1
        while step < LANES:
            seg = seg + jnp.where(lane >= step, pltpu.roll(seg, step, 1), 0.0)
            step *= 2
        segs.append(seg)
    outs = []
    for seg in segs:
        seg = seg + carry
        outs.append(seg)
        carry = seg[:, LANES - 1:LANES]
    return jnp.concatenate(outs, axis=1), carry


def _even_inproj_kernel(x_ref, wqkv_ref, wft_ref, wc_ref, bf_ref, cw_ref, st_ref,
                        q_ref, kf_ref, vf_ref, kb_ref, vb_ref, lf_ref, gc_ref, cs_ref, *rest,
                        tm, nt, emit_c):
    if emit_c:
        c_ref, prev_ref, ccarry_ref = rest
    else:
        (prev_ref,) = rest
    t = pl.program_id(1)

    @pl.when(t == 0)
    def _():
        prev_ref[...] = st_ref[0]
        if emit_c:
            ccarry_ref[...] = jnp.zeros_like(ccarry_ref)

    xb = x_ref[0].astype(BF16)
    lf = jax.nn.log_sigmoid(_dot_nt(wft_ref[...], xb) + bf_ref[...])
    lf_ref[0] = lf
    if emit_c:
        c, carry = _lane_prefix_sum(lf, ccarry_ref[...])
        c_ref[0] = c
        ccarry_ref[...] = carry

    h = _dot(xb, wc_ref[:, 0:CONV_DIM])
    bg = _dot(xb, wc_ref[:, CONV_DIM:2 * CONV_DIM])
    cg = _dot(xb, wc_ref[:, 2 * CONV_DIM:3 * CONV_DIM])
    pre = cg * h

    prev = prev_ref[...]
    row = lax.broadcasted_iota(jnp.int32, (tm, CONV_DIM), 0)
    p1 = jnp.where(row == 0, prev[7:8, :], pltpu.roll(pre, 1, 0))
    p2 = jnp.where(row == 0, prev[6:7, :],
                   jnp.where(row == 1, prev[7:8, :], pltpu.roll(pre, 2, 0)))
    cw = cw_ref[...]
    conv = cw[0:1, :] * p2 + cw[1:2, :] * p1 + cw[2:3, :] * pre
    gc_ref[0] = (bg * conv).astype(BF16)
    prev_ref[...] = pre[tm - SUBLANES:tm, :]

    q = _dot(xb, wqkv_ref[:, 0:FOX_DIM])
    q_ref[0] = (q * (LOG2E * FOX_HEAD_DIM ** -0.5)).astype(BF16)
    k = _dot(xb, wqkv_ref[:, FOX_DIM:2 * FOX_DIM])
    kf_ref[0] = k
    kb_ref[0] = k.astype(BF16)
    v = _dot(xb, wqkv_ref[:, 2 * FOX_DIM:3 * FOX_DIM])
    vf_ref[0] = v
    vb_ref[0] = v.astype(BF16)

    @pl.when(t == nt - 1)
    def _():
        cs_ref[0] = pre[tm - (CONV_WIDTH - 1):tm, :]


def _even_inproj(x, wqkv, wft, wc, bf, cw, state8, tm, emit_c):
    bn, tlen, _ = x.shape
    nt = tlen // tm
    tok = lambda w: pl.BlockSpec((1, tm, w), lambda b, t: (b, t, 0))
    tlane = pl.BlockSpec((1, FOX_HEADS, tm), lambda b, t: (b, 0, t))
    out_shape = [
        jax.ShapeDtypeStruct((bn, tlen, FOX_DIM), BF16),
        jax.ShapeDtypeStruct((bn, tlen, FOX_DIM), F32),
        jax.ShapeDtypeStruct((bn, tlen, FOX_DIM), F32),
        jax.ShapeDtypeStruct((bn, tlen, FOX_DIM), BF16),
        jax.ShapeDtypeStruct((bn, tlen, FOX_DIM), BF16),
        jax.ShapeDtypeStruct((bn, FOX_HEADS, tlen), F32),
        jax.ShapeDtypeStruct((bn, tlen, CONV_DIM), BF16),
        jax.ShapeDtypeStruct((bn, CONV_WIDTH - 1, CONV_DIM), F32),
    ]
    out_specs = [tok(FOX_DIM), tok(FOX_DIM), tok(FOX_DIM), tok(FOX_DIM), tok(FOX_DIM), tlane,
                 tok(CONV_DIM), pl.BlockSpec((1, CONV_WIDTH - 1, CONV_DIM), lambda b, t: (b, 0, 0))]
    scratch = [pltpu.VMEM((SUBLANES, CONV_DIM), F32)]
    if emit_c:
        out_shape.append(jax.ShapeDtypeStruct((bn, FOX_HEADS, tlen), F32))
        out_specs.append(tlane)
        scratch.append(pltpu.VMEM((FOX_HEADS, 1), F32))
    return pl.pallas_call(
        functools.partial(_even_inproj_kernel, tm=tm, nt=nt, emit_c=emit_c),
        grid=(bn, nt),
        in_specs=[tok(D_MODEL), _resident(wqkv.shape), _resident(wft.shape),
                  _resident(wc.shape), _resident(bf.shape), _resident(cw.shape),
                  pl.BlockSpec((1, SUBLANES, CONV_DIM), lambda b, t: (b, 0, 0))],
        out_specs=tuple(out_specs),
        out_shape=tuple(out_shape),
        scratch_shapes=scratch,
        compiler_params=_params("parallel", "arbitrary"),
        name="even_inproj",
    )(x, wqkv, wft, wc, bf, cw, state8)


def _head_mask(shape, hh):
    lane = lax.broadcasted_iota(jnp.int32, shape, 1)
    return (lane >= FOX_HEAD_DIM) if hh == 1 else (lane < FOX_HEAD_DIM)


def _fox_prompt_kernel(q_ref, k_ref, v_ref, c_ref, o_ref, vt_ref, cb_ref, *, tq, nq):
    hp = pl.program_id(1)
    i = pl.program_id(2)
    half = lax.broadcasted_iota(jnp.int32, (LANES, tq), 0) < FOX_HEAD_DIM

    @pl.when(i == 0)
    def _():
        for j in range(nq):
            rows = slice(j * tq, (j + 1) * tq)
            vt = v_ref[0, rows, :].astype(F32).T
            for hh in range(2):
                own = half if hh == 0 else jnp.logical_not(half)
                vt_ref[hh, j] = jnp.where(own, vt, 1.0).astype(BF16)
                c_j = c_ref[0, (hp * 2 + hh) * nq + j]
                rel = (c_j[:, 0:1] - c_j) * LOG2E
                cb_ref[hh, rows, :] = jnp.broadcast_to(rel, (LANES, tq)).T

    q = q_ref[0]
    qh = [jnp.where(_head_mask(q.shape, hh), q, jnp.zeros_like(q)) for hh in range(2)]
    wk = tq // FOX_KEY_SPLIT
    key = lax.broadcasted_iota(jnp.int32, (wk, tq), 0)
    qry = lax.broadcasted_iota(jnp.int32, (wk, tq), 1)

    def c_start(j, hh):
        return c_ref[0, (hp * 2 + hh) * nq + j][:, 0:1]

    def scores(j, kb, hh):
        rows = pl.ds(pl.multiple_of(j * tq + kb * wk, wk), wk)
        st = _dot_nt(k_ref[0, rows, :], qh[hh])
        return st + jnp.tile(cb_ref[hh, rows, :], (1, tq // LANES))

    def chain(tiles, carry, diagonal=False):
        blocks = [(i, kb, True) for kb in range(FOX_KEY_SPLIT)] if diagonal else []
        blocks += [(j, kb, False) for j in tiles for kb in range(FOX_KEY_SPLIT)]
        sts = [[scores(j, kb, hh) for hh in range(2)] for j, kb, _ in blocks]
        carry = [None, None] if carry is None else list(carry)
        for (j, kb, diag), st_b in zip(blocks, sts):
            for hh in range(2):
                st = st_b[hh]
                if diag:
                    st = jnp.where(key + kb * wk <= qry, st, -jnp.inf)
                    top = jnp.max(st, axis=0, keepdims=True)
                else:
                    delta = (c_start(i, hh) - c_start(j, hh)) * LOG2E
                    top = jnp.max(st, axis=0, keepdims=True) + delta
                m_new = top if carry[hh] is None else jnp.maximum(carry[hh][0], top)
                p = jnp.exp2(st - (m_new if diag else m_new - delta)).astype(BF16)
                pv = _dot(vt_ref[hh, j, :, kb * wk:(kb + 1) * wk], p)
                if carry[hh] is None:
                    carry[hh] = (m_new, pv)
                else:
                    carry[hh] = (m_new, jnp.exp2(carry[hh][0] - m_new) * carry[hh][1] + pv)
        return tuple(carry)

    carry = lax.cond(i % 2 == 1, lambda: chain([i - 1], None, True), lambda: chain([], None, True))
    carry = lax.fori_loop(0, i // 2, lambda jj, c: chain([2 * jj, 2 * jj + 1], c), carry)
    out0 = carry[0][1] / carry[0][1][FOX_HEAD_DIM:FOX_HEAD_DIM + 1, :]
    out1 = carry[1][1] / carry[1][1][0:1, :]
    o_ref[0] = jnp.where(half, out0, out1).T.astype(o_ref.dtype)


def _fox_prompt(q, k, v, c, tq):
    bn, s_len, _ = q.shape
    nq = s_len // tq
    c4 = c.reshape(bn, FOX_HEADS * nq, 1, tq)
    qspec = pl.BlockSpec((1, tq, LANES), lambda b, h, i: (b, i, h))
    kspec = pl.BlockSpec((1, s_len, LANES), lambda b, h, i: (b, 0, h))
    return pl.pallas_call(
        functools.partial(_fox_prompt_kernel, tq=tq, nq=nq),
        grid=(bn, FOX_DIM // LANES, nq),
        in_specs=[qspec, kspec, kspec,
                  pl.BlockSpec((1, FOX_HEADS * nq, 1, tq), lambda b, h, i: (b, 0, 0, 0))],
        out_specs=qspec,
        out_shape=jax.ShapeDtypeStruct((bn, s_len, FOX_DIM), BF16),
        scratch_shapes=[pltpu.VMEM((2, nq, LANES, tq), BF16),
                        pltpu.VMEM((2, s_len, LANES), F32)],
        compiler_params=_params("parallel", "parallel", "arbitrary"),
        name="fox_prompt",
    )(q, k, v, c4)


def _fox_sample_kernel(q_ref, kn_ref, vn_ref, kc_ref, vc_ref, lf_ref, o_ref, *, t_new, p_len):
    n_new = lf_ref.shape[-1] - p_len
    zeros = jnp.zeros((n_new - t_new, LANES), BF16)
    row = lax.broadcasted_iota(jnp.int32, (t_new, n_new), 0)
    col = lax.broadcasted_iota(jnp.int32, (t_new, n_new), 1)
    c_all, _ = _lane_prefix_sum(lf_ref[0], jnp.zeros((FOX_HEADS, 1), F32))
    for hp in range(FOX_DIM // LANES):
        lanes = slice(hp * LANES, (hp + 1) * LANES)
        q = q_ref[0, :, lanes]
        kn = jnp.concatenate([kn_ref[0, :, lanes], zeros], axis=0)
        vn = jnp.concatenate([vn_ref[0, :, lanes], zeros], axis=0)
        kc = kc_ref[0, :, lanes].astype(BF16)
        vc = vc_ref[0, :, lanes].astype(BF16)
        outs = []
        for hh in range(2):
            head = hp * 2 + hh
            qh = jnp.where(_head_mask(q.shape, hh), q, jnp.zeros_like(q))
            c_row = c_all[head:head + 1, :]
            bias = (c_row[:, p_len - 1:p_len] - c_row) * LOG2E
            s_c = _dot_nt(qh, kc) + bias[:, 0:p_len]
            s_n = jnp.where(col <= row, _dot_nt(qh, kn) + bias[:, p_len:], -jnp.inf)
            m = jnp.maximum(jnp.max(s_c, axis=-1, keepdims=True), jnp.max(s_n, axis=-1, keepdims=True))
            p_c = jnp.exp2(s_c - m)
            p_n = jnp.exp2(s_n - m)
            l = jnp.sum(p_c, axis=-1, keepdims=True) + jnp.sum(p_n, axis=-1, keepdims=True)
            outs.append((_dot(p_c.astype(BF16), vc) + _dot(p_n.astype(BF16), vn)) / l)
        o_ref[0, :, lanes] = jnp.where(_head_mask(outs[0].shape, 0), outs[0], outs[1]).astype(o_ref.dtype)


def _fox_sample(q, kn, vn, kc, vc, lf_all):
    bn, t_new, _ = q.shape
    p_len = kc.shape[1]
    n_keys = lf_all.shape[-1]
    qspec = pl.BlockSpec((1, t_new, FOX_DIM), lambda b: (b, 0, 0))
    cspec = pl.BlockSpec((1, p_len, FOX_DIM), lambda b: (b, 0, 0))
    return pl.pallas_call(
        functools.partial(_fox_sample_kernel, t_new=t_new, p_len=p_len),
        grid=(bn,),
        in_specs=[qspec, qspec, qspec, cspec, cspec,
                  pl.BlockSpec((1, FOX_HEADS, n_keys), lambda b: (b, 0, 0))],
        out_specs=qspec,
        out_shape=jax.ShapeDtypeStruct((bn, t_new, FOX_DIM), BF16),
        compiler_params=_params("parallel"),
        name="fox_sample",
    )(q, kn, vn, kc, vc, lf_all)


def _gmlp_kernel(x_ref, win_ref, bin_ref, ng_ref, nb_ref, ws_ref, bst_ref, wout_ref,
                 g_ref, b_ref, y_ref, *rest, tm, alpha, emit_vn, n_sub):
    if emit_vn:
        vn_ref, u_scr, vnb_scr, gated_scr = rest
    else:
        u_scr, vnb_scr, gated_scr = rest
    row = lax.broadcasted_iota(jnp.int32, (GMLP_CHUNK, GMLP_CHUNK), 0)
    col = lax.broadcasted_iota(jnp.int32, (GMLP_CHUNK, GMLP_CHUNK), 1)
    gd = GMLP_DIM // GMLP_GROUPS
    rows_per = tm // n_sub
    for sb in range(n_sub):
        rows = slice(sb * rows_per, (sb + 1) * rows_per)
        x = x_ref[rows, :]
        xb = x.astype(BF16)
        v = _gelu(_dot(xb, win_ref[:, GMLP_DIM:]) + bin_ref[:, GMLP_DIM:])
        vn = _layer_norm(v, ng_ref[...], nb_ref[...])
        if emit_vn:
            vn_ref[rows, :] = vn
        vnb_scr[rows, :] = vn.astype(BF16)
        u_scr[rows, :] = _gelu(_dot(xb, win_ref[:, 0:GMLP_DIM]) + bin_ref[:, 0:GMLP_DIM])

        chunks = range(sb * rows_per // GMLP_CHUNK, (sb + 1) * rows_per // GMLP_CHUNK)
        for g in range(GMLP_GROUPS):
            wm = jnp.where(col <= row, ws_ref[g], 0.0).astype(BF16)
            bias = bst_ref[:, g:g + 1]
            cs = slice(g * gd, (g + 1) * gd)
            for r in range(chunks.start, chunks.stop, 2):
                rss = [slice(rr * GMLP_CHUNK, (rr + 1) * GMLP_CHUNK) for rr in range(r, min(r + 2, chunks.stop))]
                mixed = _dot(wm, jnp.concatenate([vnb_scr[rs, cs] for rs in rss], axis=1)) + bias
                for n, rs in enumerate(rss):
                    gated_scr[rs, cs] = (u_scr[rs, cs] * mixed[:, n * gd:(n + 1) * gd]).astype(BF16)
        out = _dot(gated_scr[rows, :], wout_ref[...])
        y_ref[rows, :] = _layer_norm(alpha * x + out, g_ref[...], b_ref[...])


def _gmlp_mixer(x, layer, win, bin_, ng, nb, ws, bst, wout, g, b, alpha, tm, emit_vn):
    n = x.shape[0]
    tok = pl.BlockSpec((tm, D_MODEL), lambda i: (i, 0))
    out_shape = [jax.ShapeDtypeStruct((n, D_MODEL), F32)]
    out_specs = [tok]
    if emit_vn:
        out_shape.append(jax.ShapeDtypeStruct((n, GMLP_DIM), F32))
        out_specs.append(tok)
    res = pl.pallas_call(
        functools.partial(_gmlp_kernel, tm=tm, alpha=alpha, emit_vn=emit_vn, n_sub=max(1, tm // TOKEN_TILE)),
        grid=(n // tm,),
        in_specs=[tok, _resident(win.shape, layer), _resident(bin_.shape), _resident(ng.shape),
                  _resident(nb.shape), _resident(ws.shape, layer), _resident(bst.shape),
                  _resident(wout.shape, layer), _resident(g.shape), _resident(b.shape)],
        out_specs=tuple(out_specs),
        out_shape=tuple(out_shape),
        scratch_shapes=[pltpu.VMEM((tm, GMLP_DIM), F32), pltpu.VMEM((tm, GMLP_DIM), BF16),
                        pltpu.VMEM((tm, GMLP_DIM), BF16)],
        compiler_params=_params("parallel"),
        name="gmlp_mixer",
    )(x, win, bin_, ng, nb, ws, bst, wout, g, b)
    return res if emit_vn else (res[0], None)


def _mem_attn_kernel(x_ref, *refs, alpha, fuse_even, project_kv, n_sub):
    if fuse_even:
        att_ref, gc_ref, wmix_ref, g0_ref, b0_ref = refs[:5]
        refs = refs[5:]
    if project_kv:
        (mem_ref, wk_ref, wv_ref, wq_ref, wo_ref, g_ref, b_ref,
         y_ref, mk_ref, mv_ref, mkb_ref, mvb_ref) = refs
    else:
        mk_ref, mv_ref, wq_ref, wo_ref, g_ref, b_ref, y_ref, mkb_ref, mvb_ref = refs

    @pl.when(pl.program_id(1) == 0)
    def _():
        if project_kv:
            mb = mem_ref[0].astype(BF16)
            k = _dot(mb, wk_ref[...])
            v = _dot(mb, wv_ref[...])
        for h in range(MEM_HEADS):
            hs = slice(h * MEM_HEAD_DIM, (h + 1) * MEM_HEAD_DIM)
            if project_kv:
                mk_ref[0, :, h, :] = k[:, hs]
                mv_ref[0, :, h, :] = v[:, hs]
                mkb_ref[h] = k[:, hs].astype(BF16)
                mvb_ref[h] = v[:, hs].astype(BF16)
            else:
                mkb_ref[h] = mk_ref[0, 0, :, h, :].astype(BF16)
                mvb_ref[h] = mv_ref[0, 0, :, h, :].astype(BF16)

    rows_per = x_ref.shape[1] // n_sub
    for sb in range(n_sub):
        rows = slice(sb * rows_per, (sb + 1) * rows_per)
        x = x_ref[0, rows, :]
        if fuse_even:
            mix = _dot(att_ref[0, rows, :], wmix_ref[0:FOX_DIM, :]) + _dot(gc_ref[0, rows, :], wmix_ref[FOX_DIM:, :])
            x = _layer_norm(alpha * x + mix, g0_ref[...], b0_ref[...])
        q = _dot(x.astype(BF16), wq_ref[...]) * (LOG2E * MEM_HEAD_DIM ** -0.5)
        qb = q.astype(BF16)
        heads = []
        for h in range(MEM_HEADS):
            hs = slice(h * MEM_HEAD_DIM, (h + 1) * MEM_HEAD_DIM)
            s = _dot_nt(qb[:, hs], mkb_ref[h])
            m = jnp.max(s, axis=-1, keepdims=True)
            p = jnp.exp2(s - m)
            l = jnp.sum(p, axis=-1, keepdims=True)
            heads.append((_dot(p.astype(BF16), mvb_ref[h]) / l).astype(BF16))
        o = jnp.concatenate(heads, axis=-1)
        y_ref[0, rows, :] = _layer_norm(alpha * x + _dot(o, wo_ref[...]), g_ref[...], b_ref[...])


def _mem_attn(x, kv, layer, wq, wo, g, b, alpha, tm, even=None):
    bn, tlen, _ = x.shape
    project_kv = len(kv) == 3
    mlen = kv[0].shape[1] if project_kv else kv[0].shape[2]
    tok = lambda w: pl.BlockSpec((1, tm, w), lambda bb, t: (bb, t, 0))
    if project_kv:
        kv_specs = [pl.BlockSpec((1, mlen, D_MODEL), lambda bb, t: (bb, 0, 0)),
                    _resident(kv[1].shape, layer), _resident(kv[2].shape, layer)]
    else:
        kv_specs = [pl.BlockSpec((1, 1, mlen, MEM_HEADS, MEM_HEAD_DIM), lambda bb, t: (layer, bb, 0, 0, 0))] * 2
    pre_args, pre_specs = [], []
    if even is not None:
        att, gc, wmix, e, g0, b0 = even
        pre_args = [att, gc, wmix, g0, b0]
        pre_specs = [tok(FOX_DIM), tok(CONV_DIM), _resident(wmix.shape, e), _resident(g0.shape),
                     _resident(b0.shape)]
    out_specs = [tok(D_MODEL)]
    out_shape = [jax.ShapeDtypeStruct((bn, tlen, D_MODEL), F32)]
    if project_kv:
        out_specs += [pl.BlockSpec((1, mlen, MEM_HEADS, MEM_HEAD_DIM), lambda bb, t: (bb, 0, 0, 0))] * 2
        out_shape += [jax.ShapeDtypeStruct((bn, mlen, MEM_HEADS, MEM_HEAD_DIM), F32)] * 2
    res = pl.pallas_call(
        functools.partial(_mem_attn_kernel, alpha=alpha, fuse_even=even is not None,
                          project_kv=project_kv, n_sub=max(1, tm // TOKEN_TILE)),
        grid=(bn, tlen // tm),
        in_specs=[tok(D_MODEL)] + pre_specs + kv_specs + [_resident(wq.shape, layer), _resident(wo.shape, layer),
                                                          _resident(g.shape), _resident(b.shape)],
        out_specs=tuple(out_specs),
        out_shape=tuple(out_shape),
        scratch_shapes=[pltpu.VMEM((MEM_HEADS, mlen, MEM_HEAD_DIM), BF16),
                        pltpu.VMEM((MEM_HEADS, mlen, MEM_HEAD_DIM), BF16)],
        compiler_params=_params("parallel", "arbitrary"),
        name="mem_attn",
    )(x, *pre_args, *kv, wq, wo, g, b)
    return res if project_kv else res[0]


def _ffn_kernel(x_ref, wg_ref, wu_ref, wd_ref, g_ref, b_ref, y_ref, h_scr, *, alpha, n_sub):
    d_ff = wg_ref.shape[1]
    rows_per = x_ref.shape[0] // n_sub
    for s in range(n_sub):
        rows = slice(s * rows_per, (s + 1) * rows_per)
        x = x_ref[rows, :]
        xb = x.astype(BF16)
        for c in range(d_ff // FFN_CHUNK):
            cols = slice(c * FFN_CHUNK, (c + 1) * FFN_CHUNK)
            gate = _dot(xb, wg_ref[:, cols])
            up = _dot(xb, wu_ref[:, cols])
            h_scr[rows, cols] = (gate * jax.nn.sigmoid(gate) * up).astype(BF16)
        y = alpha * x + _dot(h_scr[rows, :], wd_ref[...])
        y_ref[rows, :] = _layer_norm(y, g_ref[...], b_ref[...])


def _ffn(x, layer, wg, wu, wd, g, b, alpha, tm):
    n = x.shape[0]
    d_ff = wg.shape[-1]
    assert d_ff % FFN_CHUNK == 0
    tok = pl.BlockSpec((tm, D_MODEL), lambda i: (i, 0))
    return pl.pallas_call(
        functools.partial(_ffn_kernel, alpha=alpha, n_sub=max(1, tm // TOKEN_TILE)),
        grid=(n // tm,),
        in_specs=[tok, _resident(wg.shape, layer), _resident(wu.shape, layer), _resident(wd.shape, layer),
                  _resident(g.shape), _resident(b.shape)],
        out_specs=tok,
        out_shape=jax.ShapeDtypeStruct((n, D_MODEL), F32),
        scratch_shapes=[pltpu.VMEM((tm, d_ff), BF16)],
        compiler_params=_params("parallel"),
        name="swiglu_ffn",
    )(x, wg, wu, wd, g, b)


def kernel(x_prompt, x_sample, cache_fox_k, cache_fox_v, cache_fox_logf, state_conv, cache_mem_k, cache_mem_v, mem_prompt, w_in_even, b_forget, conv_w, w_out_even, w_in_odd, b_in_odd, gmlp_norm_g, gmlp_norm_b, gmlp_w_s, gmlp_b_s, w_out_odd, mem_w_q, mem_w_k, mem_w_v, mem_w_o, ffn_w_gate, ffn_w_up, ffn_w_down, ln_g, ln_b):
    depth = ln_g.shape[0]
    alpha = float((2 * depth) ** 0.25)
    bp, s_len, d = x_prompt.shape
    bs, t_new, _ = x_sample.shape
    p_len = cache_fox_k.shape[2]
    row2 = lambda a: a.reshape(1, -1)

    (w_out_even_b, w_in_odd_b, w_out_odd_b, mem_wq_b, mem_wo_b, mem_wk_b, mem_wv_b,
     ffn_wg_b, ffn_wu_b, ffn_wd_b) = _to_bf16([w_out_even, w_in_odd, w_out_odd, mem_w_q, mem_w_o, mem_w_k,
                                               mem_w_v, ffn_w_gate, ffn_w_up, ffn_w_down])

    yp, ys = x_prompt, x_sample
    fk_p, fv_p, fl_p, cs_p, mk_p, mv_p = [], [], [], [], [], []
    fk_s, fv_s, fl_s, cs_s, gv_s = [], [], [], [], []
    for layer in range(depth):
        lg, lb = ln_g[layer], ln_b[layer]
        even_p = even_s = None
        if layer % 2 == 0:
            e = layer // 2
            w_in = w_in_even[e].astype(BF16)
            qkv_w = w_in[:, 0:3 * FOX_DIM]
            f_wt = w_in[:, 3 * FOX_DIM:3 * FOX_DIM + FOX_HEADS].T
            c_w = w_in[:, 3 * FOX_DIM + FOX_HEADS:]
            f_b = b_forget[e].reshape(FOX_HEADS, 1)
            cw = conv_w[e]

            st0 = jnp.zeros((bp, SUBLANES, CONV_DIM), F32)
            q, kf, vf, kb, vb, lf, gc, cst, c = _even_inproj(yp, qkv_w, f_wt, c_w, f_b, cw, st0,
                                                             TOKEN_TILE, True)
            att = _fox_prompt(q, kb, vb, c, FOX_TILE)
            even_p = (att, gc, w_out_even_b, e, row2(lg[0]), row2(lb[0]))
            fk_p.append(kf.reshape(bp, s_len, FOX_HEADS, FOX_HEAD_DIM))
            fv_p.append(vf.reshape(bp, s_len, FOX_HEADS, FOX_HEAD_DIM))
            fl_p.append(jnp.swapaxes(lf, 1, 2))
            cs_p.append(cst)

            st = jnp.pad(state_conv[e], ((0, 0), (SUBLANES - (CONV_WIDTH - 1), 0), (0, 0)))
            q, kf, vf, kb, vb, lf, gc, cst = _even_inproj(ys, qkv_w, f_wt, c_w, f_b, cw, st, t_new, False)
            pad = -(p_len + t_new) % LANES
            lf_all = jnp.concatenate([jnp.swapaxes(cache_fox_logf[e], 1, 2), lf,
                                      jnp.zeros((bs, FOX_HEADS, pad), F32)], axis=2)
            att = _fox_sample(q, kb, vb, cache_fox_k[e].reshape(bs, p_len, FOX_DIM),
                              cache_fox_v[e].reshape(bs, p_len, FOX_DIM), lf_all)
            even_s = (att, gc, w_out_even_b, e, row2(lg[0]), row2(lb[0]))
            fk_s.append(kf.reshape(bs, t_new, FOX_HEADS, FOX_HEAD_DIM))
            fv_s.append(vf.reshape(bs, t_new, FOX_HEADS, FOX_HEAD_DIM))
            fl_s.append(jnp.swapaxes(lf, 1, 2))
            cs_s.append(cst)
        else:
            o = layer // 2
            args = (o, w_in_odd_b, row2(b_in_odd[o]), row2(gmlp_norm_g[o]), row2(gmlp_norm_b[o]),
                    gmlp_w_s, gmlp_b_s[o].T, w_out_odd_b, row2(lg[0]), row2(lb[0]), alpha)
            yp, _ = _gmlp_mixer(yp.reshape(-1, d), *args, 2 * TOKEN_TILE, False)
            yp = yp.reshape(bp, s_len, d)
            ys_pad = jnp.pad(ys, ((0, 0), (0, GMLP_CHUNK - t_new), (0, 0))).reshape(-1, d)
            ys_pad, vn = _gmlp_mixer(ys_pad, *args, TOKEN_TILE, True)
            ys = ys_pad.reshape(bs, GMLP_CHUNK, d)[:, :t_new]
            gv_s.append(vn.reshape(bs, GMLP_CHUNK, GMLP_DIM)[:, :t_new])

        yp, mk, mv = _mem_attn(yp, (mem_prompt, mem_wk_b, mem_wv_b), layer, mem_wq_b, mem_wo_b,
                               row2(lg[1]), row2(lb[1]), alpha, 2 * TOKEN_TILE, even_p)
        mk_p.append(mk)
        mv_p.append(mv)
        ys = _mem_attn(ys, (cache_mem_k, cache_mem_v), layer, mem_wq_b, mem_wo_b, row2(lg[1]), row2(lb[1]),
                       alpha, t_new, even_s)

        ffn_args = (layer, ffn_wg_b, ffn_wu_b, ffn_wd_b, row2(lg[2]), row2(lb[2]), alpha)
        yp = _ffn(yp.reshape(-1, d), *ffn_args, FFN_ROWS_PER_STEP).reshape(bp, s_len, d)
        ys = _ffn(ys.reshape(-1, d), *ffn_args, bs * t_new).reshape(bs, t_new, d)

    return (yp, ys, jnp.stack(fk_p), jnp.stack(fv_p), jnp.stack(fl_p), jnp.stack(cs_p),
            jnp.stack(mk_p), jnp.stack(mv_p),
            jnp.stack(fk_s), jnp.stack(fv_s), jnp.stack(fl_s), jnp.stack(cs_s), jnp.stack(gv_s))
```

```python
import functools

import jax
import jax.numpy as jnp
from jax import lax
from jax.experimental import pallas as pl
from jax.experimental.pallas import tpu as pltpu

F32 = jnp.float32
BF16 = jnp.bfloat16

D_MODEL = 1024
FOX_HEADS = 8
FOX_HEAD_DIM = 64
FOX_DIM = FOX_HEADS * FOX_HEAD_DIM
CONV_DIM = 512
CONV_WIDTH = 3
GMLP_CHUNK = 128
GMLP_GROUPS = 8
GMLP_DIM = D_MODEL
MEM_HEADS = 4
MEM_HEAD_DIM = 128
MEM_DIM = MEM_HEADS * MEM_HEAD_DIM
LN_EPS = 1e-5

LANES = 128
SUBLANES = 8
VMEM_LIMIT_BYTES = 56 * 1024 * 1024
TOKEN_TILE = 512
FOX_TILE = 512
FOX_BLOCKS = 2
FFN_CHUNK = 256
FFN_ROWS_PER_STEP = 2 * TOKEN_TILE
LOG2E = 1.4426950408889634
CAST_BLOCK_BYTES = 1024 * 1024


def _params(*semantics):
    return pltpu.CompilerParams(dimension_semantics=semantics,
                                vmem_limit_bytes=VMEM_LIMIT_BYTES)


def _resident(shape, layer=None):
    if layer is None:
        nd = len(shape)
        return pl.BlockSpec(shape, lambda *_: (0,) * nd, pipeline_mode=pl.Buffered(1))
    nd = len(shape) - 1
    return pl.BlockSpec((None,) + tuple(shape[1:]), lambda *_: (layer,) + (0,) * nd,
                        pipeline_mode=pl.Buffered(1))


def _dot(a, b):
    return jnp.dot(a, b, preferred_element_type=F32)


def _dot_nt(a, b):
    return lax.dot_general(a, b, (((1,), (1,)), ((), ())), preferred_element_type=F32)


def _layer_norm(y, g, b):
    mu = jnp.mean(y, axis=-1, keepdims=True)
    d = y - mu
    var = jnp.mean(d * d, axis=-1, keepdims=True)
    return d * lax.rsqrt(var + LN_EPS) * g + b


def _gelu(z):
    return 0.5 * z * (1.0 + lax.erf(z * (2.0 ** -0.5)))


def _cast_kernel(*refs, n_blocks):
    n = len(n_blocks)
    i = pl.program_id(0)
    for k in range(n):
        @pl.when(i < n_blocks[k])
        def _(k=k):
            refs[n + k][...] = refs[k][...].astype(BF16)


def _to_bf16(ws):
    flat, specs, n_blocks = [], [], []
    for w in ws:
        n = w.shape[-1]
        rows = w.size // n
        blk = rows
        while blk * n * w.dtype.itemsize > CAST_BLOCK_BYTES and blk % (4 * SUBLANES) == 0:
            blk //= 2
        nb = rows // blk
        flat.append(w.reshape(rows, n))
        specs.append(pl.BlockSpec((blk, n), lambda i, nb=nb: (jnp.minimum(i, nb - 1), 0)))
        n_blocks.append(nb)
    outs = pl.pallas_call(
        functools.partial(_cast_kernel, n_blocks=tuple(n_blocks)),
        grid=(max(n_blocks),), in_specs=specs, out_specs=tuple(specs),
        out_shape=tuple(jax.ShapeDtypeStruct(f.shape, BF16) for f in flat),
        compiler_params=_params("arbitrary"), name="to_bf16",
    )(*flat)
    return [o.reshape(w.shape) for o, w in zip(outs, ws)]


def _lane_prefix_sum(x, carry):
    rows, n = x.shape
    lane = lax.broadcasted_iota(jnp.int32, (rows, LANES), 1)
    segs = []
    for j in range(n // LANES):
        seg = x[:, j * LANES:(j + 1) * LANES]
        step = 1
        while step < LANES:
            seg = seg + jnp.where(lane >= step, pltpu.roll(seg, step, 1), 0.0)
            step *= 2
        segs.append(seg)
    outs = []
    for seg in segs:
        seg = seg + carry
        outs.append(seg)
        carry = seg[:, LANES - 1:LANES]
    return jnp.concatenate(outs, axis=1), carry


def _even_inproj_kernel(x_ref, wqkv_ref, wft_ref, wc_ref, bf_ref, cw_ref, st_ref,
                        q_ref, kf_ref, vf_ref, kb_ref, vb_ref, lf_ref, gc_ref, cs_ref, *rest,
                        tm, nt, emit_c):
    if emit_c:
        c_ref, prev_ref, ccarry_ref = rest
    else:
        (prev_ref,) = rest
    t = pl.program_id(1)

    @pl.when(t == 0)
    def _():
        prev_ref[...] = st_ref[0]
        if emit_c:
            ccarry_ref[...] = jnp.zeros_like(ccarry_ref)

    xb = x_ref[0].astype(BF16)
    lf = jax.nn.log_sigmoid(_dot_nt(wft_ref[...], xb) + bf_ref[...])
    lf_ref[0] = lf
    if emit_c:
        c, carry = _lane_prefix_sum(lf, ccarry_ref[...])
        c_ref[0] = c
        ccarry_ref[...] = carry

    h = _dot(xb, wc_ref[:, 0:CONV_DIM])
    bg = _dot(xb, wc_ref[:, CONV_DIM:2 * CONV_DIM])
    cg = _dot(xb, wc_ref[:, 2 * CONV_DIM:3 * CONV_DIM])
    pre = cg * h

    prev = prev_ref[...]
    row = lax.broadcasted_iota(jnp.int32, (tm, CONV_DIM), 0)
    p1 = jnp.where(row == 0, prev[7:8, :], pltpu.roll(pre, 1, 0))
    p2 = jnp.where(row == 0, prev[6:7, :],
                   jnp.where(row == 1, prev[7:8, :], pltpu.roll(pre, 2, 0)))
    cw = cw_ref[...]
    conv = cw[0:1, :] * p2 + cw[1:2, :] * p1 + cw[2:3, :] * pre
    gc_ref[0] = (bg * conv).astype(BF16)
    prev_ref[...] = pre[tm - SUBLANES:tm, :]

    q = _dot(xb, wqkv_ref[:, 0:FOX_DIM])
    q_ref[0] = (q * (LOG2E * FOX_HEAD_DIM ** -0.5)).astype(BF16)
    k = _dot(xb, wqkv_ref[:, FOX_DIM:2 * FOX_DIM])
    kf_ref[0] = k
    kb_ref[0] = k.astype(BF16)
    v = _dot(xb, wqkv_ref[:, 2 * FOX_DIM:3 * FOX_DIM])
    vf_ref[0] = v
    vb_ref[0] = v.astype(BF16)

    @pl.when(t == nt - 1)
    def _():
        cs_ref[0] = pre[tm - (CONV_WIDTH - 1):tm, :]


def _even_inproj(x, wqkv, wft, wc, bf, cw, state8, tm, emit_c):
    bn, tlen, _ = x.shape
    nt = tlen // tm
    tok = lambda w: pl.BlockSpec((1, tm, w), lambda b, t: (b, t, 0))
    tlane = pl.BlockSpec((1, FOX_HEADS, tm), lambda b, t: (b, 0, t))
    out_shape = [
        jax.ShapeDtypeStruct((bn, tlen, FOX_DIM), BF16),
        jax.ShapeDtypeStruct((bn, tlen, FOX_DIM), F32),
        jax.ShapeDtypeStruct((bn, tlen, FOX_DIM), F32),
        jax.ShapeDtypeStruct((bn, tlen, FOX_DIM), BF16),
        jax.ShapeDtypeStruct((bn, tlen, FOX_DIM), BF16),
        jax.ShapeDtypeStruct((bn, FOX_HEADS, tlen), F32),
        jax.ShapeDtypeStruct((bn, tlen, CONV_DIM), BF16),
        jax.ShapeDtypeStruct((bn, CONV_WIDTH - 1, CONV_DIM), F32),
    ]
    out_specs = [tok(FOX_DIM), tok(FOX_DIM), tok(FOX_DIM), tok(FOX_DIM), tok(FOX_DIM), tlane,
                 tok(CONV_DIM), pl.BlockSpec((1, CONV_WIDTH - 1, CONV_DIM), lambda b, t: (b, 0, 0))]
    scratch = [pltpu.VMEM((SUBLANES, CONV_DIM), F32)]
    if emit_c:
        out_shape.append(jax.ShapeDtypeStruct((bn, FOX_HEADS, tlen), F32))
        out_specs.append(tlane)
        scratch.append(pltpu.VMEM((FOX_HEADS, 1), F32))
    return pl.pallas_call(
        functools.partial(_even_inproj_kernel, tm=tm, nt=nt, emit_c=emit_c),
        grid=(bn, nt),
        in_specs=[tok(D_MODEL), _resident(wqkv.shape), _resident(wft.shape),
                  _resident(wc.shape), _resident(bf.shape), _resident(cw.shape),
                  pl.BlockSpec((1, SUBLANES, CONV_DIM), lambda b, t: (b, 0, 0))],
        out_specs=tuple(out_specs),
        out_shape=tuple(out_shape),
        scratch_shapes=scratch,
        compiler_params=_params("parallel", "arbitrary"),
        name="even_inproj",
    )(x, wqkv, wft, wc, bf, cw, state8)


def _head_mask(shape, hh):
    lane = lax.broadcasted_iota(jnp.int32, shape, 1)
    return (lane >= FOX_HEAD_DIM) if hh == 1 else (lane < FOX_HEAD_DIM)


def _fox_prompt_kernel(q_ref, k_ref, v_ref, c_ref, o_ref, vt_ref, cb_ref, *, tq, nq):
    hp = pl.program_id(1)
    i = pl.program_id(2)
    half = lax.broadcasted_iota(jnp.int32, (LANES, tq), 0) < FOX_HEAD_DIM

    @pl.when(i == 0)
    def _():
        for j in range(nq):
            rows = slice(j * tq, (j + 1) * tq)
            vt = v_ref[0, rows, :].astype(F32).T
            for hh in range(2):
                own = half if hh == 0 else jnp.logical_not(half)
                vt_ref[hh, j] = jnp.where(own, vt, 1.0).astype(BF16)
                c_j = c_ref[0, (hp * 2 + hh) * nq + j]
                rel = (c_j[:, 0:1] - c_j) * LOG2E
                cb_ref[hh, rows, :] = jnp.broadcast_to(rel, (LANES, tq)).T

    q = q_ref[0]
    qh = [jnp.where(_head_mask(q.shape, hh), q, jnp.zeros_like(q)) for hh in range(2)]
    wb = tq // FOX_BLOCKS
    key = lax.broadcasted_iota(jnp.int32, (wb, wb), 0)
    qry = lax.broadcasted_iota(jnp.int32, (wb, wb), 1)
    streams = [(hh, qb) for hh in range(2) for qb in range(FOX_BLOCKS)]

    def c_start(j, hh):
        return c_ref[0, (hp * 2 + hh) * nq + j][:, 0:1]

    def scores(j, kb, hh, qb):
        rows = pl.ds(pl.multiple_of(j * tq + kb * wb, wb), wb)
        st = _dot_nt(k_ref[0, rows, :], qh[hh][qb * wb:(qb + 1) * wb, :])
        return st + jnp.tile(cb_ref[hh, rows, :], (1, wb // LANES))

    def chain(tiles, carry, diagonal=False):
        blocks = [(i, kb, True) for kb in range(FOX_BLOCKS)] if diagonal else []
        blocks += [(j, kb, False) for j in tiles for kb in range(FOX_BLOCKS)]
        live = lambda kb, qb, diag: not diag or kb <= qb
        sts = [[scores(j, kb, hh, qb) if live(kb, qb, diag) else None for hh, qb in streams]
               for j, kb, diag in blocks]
        carry = [None] * len(streams) if carry is None else list(carry)
        for (j, kb, diag), st_b in zip(blocks, sts):
            for n, (hh, qb) in enumerate(streams):
                st = st_b[n]
                if st is None:
                    continue
                if diag:
                    if kb == qb:
                        st = jnp.where(key <= qry, st, -jnp.inf)
                    top = jnp.max(st, axis=0, keepdims=True)
                else:
                    delta = (c_start(i, hh) - c_start(j, hh)) * LOG2E
                    top = jnp.max(st, axis=0, keepdims=True) + delta
                m_new = top if carry[n] is None else jnp.maximum(carry[n][0], top)
                p = jnp.exp2(st - (m_new if diag else m_new - delta)).astype(BF16)
                pv = _dot(vt_ref[hh, j, :, kb * wb:(kb + 1) * wb], p)
                if carry[n] is None:
                    carry[n] = (m_new, pv)
                else:
                    carry[n] = (m_new, jnp.exp2(carry[n][0] - m_new) * carry[n][1] + pv)
        return tuple(carry)

    carry = lax.cond(i % 2 == 1, lambda: chain([i - 1], None, True), lambda: chain([], None, True))
    carry = lax.fori_loop(0, i // 2, lambda jj, c: chain([2 * jj, 2 * jj + 1], c), carry)
    accs = [jnp.concatenate([carry[hh * FOX_BLOCKS + qb][1] for qb in range(FOX_BLOCKS)], axis=1)
            for hh in range(2)]
    out0 = accs[0] / accs[0][FOX_HEAD_DIM:FOX_HEAD_DIM + 1, :]
    out1 = accs[1] / accs[1][0:1, :]
    o_ref[0] = jnp.where(half, out0, out1).T.astype(o_ref.dtype)


def _fox_prompt(q, k, v, c, tq):
    bn, s_len, _ = q.shape
    nq = s_len // tq
    c4 = c.reshape(bn, FOX_HEADS * nq, 1, tq)
    qspec = pl.BlockSpec((1, tq, LANES), lambda b, h, i: (b, i, h))
    kspec = pl.BlockSpec((1, s_len, LANES), lambda b, h, i: (b, 0, h))
    return pl.pallas_call(
        functools.partial(_fox_prompt_kernel, tq=tq, nq=nq),
        grid=(bn, FOX_DIM // LANES, nq),
        in_specs=[qspec, kspec, kspec,
                  pl.BlockSpec((1, FOX_HEADS * nq, 1, tq), lambda b, h, i: (b, 0, 0, 0))],
        out_specs=qspec,
        out_shape=jax.ShapeDtypeStruct((bn, s_len, FOX_DIM), BF16),
        scratch_shapes=[pltpu.VMEM((2, nq, LANES, tq), BF16),
                        pltpu.VMEM((2, s_len, LANES), F32)],
        compiler_params=_params("parallel", "parallel", "arbitrary"),
        name="fox_prompt",
    )(q, k, v, c4)


def _fox_sample_kernel(q_ref, kn_ref, vn_ref, kc_ref, vc_ref, lf_ref, o_ref, *, t_new, p_len):
    n_new = lf_ref.shape[-1] - p_len
    zeros = jnp.zeros((n_new - t_new, LANES), BF16)
    row = lax.broadcasted_iota(jnp.int32, (t_new, n_new), 0)
    col = lax.broadcasted_iota(jnp.int32, (t_new, n_new), 1)
    c_all, _ = _lane_prefix_sum(lf_ref[0], jnp.zeros((FOX_HEADS, 1), F32))
    for hp in range(FOX_DIM // LANES):
        lanes = slice(hp * LANES, (hp + 1) * LANES)
        q = q_ref[0, :, lanes]
        kn = jnp.concatenate([kn_ref[0, :, lanes], zeros], axis=0)
        vn = jnp.concatenate([vn_ref[0, :, lanes], zeros], axis=0)
        kc = kc_ref[0, :, lanes].astype(BF16)
        vc = vc_ref[0, :, lanes].astype(BF16)
        outs = []
        for hh in range(2):
            head = hp * 2 + hh
            qh = jnp.where(_head_mask(q.shape, hh), q, jnp.zeros_like(q))
            c_row = c_all[head:head + 1, :]
            bias = (c_row[:, p_len - 1:p_len] - c_row) * LOG2E
            s_c = _dot_nt(qh, kc) + bias[:, 0:p_len]
            s_n = jnp.where(col <= row, _dot_nt(qh, kn) + bias[:, p_len:], -jnp.inf)
            m = jnp.maximum(jnp.max(s_c, axis=-1, keepdims=True), jnp.max(s_n, axis=-1, keepdims=True))
            p_c = jnp.exp2(s_c - m)
            p_n = jnp.exp2(s_n - m)
            l = jnp.sum(p_c, axis=-1, keepdims=True) + jnp.sum(p_n, axis=-1, keepdims=True)
            outs.append((_dot(p_c.astype(BF16), vc) + _dot(p_n.astype(BF16), vn)) / l)
        o_ref[0, :, lanes] = jnp.where(_head_mask(outs[0].shape, 0), outs[0], outs[1]).astype(o_ref.dtype)


def _fox_sample(q, kn, vn, kc, vc, lf_all):
    bn, t_new, _ = q.shape
    p_len = kc.shape[1]
    n_keys = lf_all.shape[-1]
    qspec = pl.BlockSpec((1, t_new, FOX_DIM), lambda b: (b, 0, 0))
    cspec = pl.BlockSpec((1, p_len, FOX_DIM), lambda b: (b, 0, 0))
    return pl.pallas_call(
        functools.partial(_fox_sample_kernel, t_new=t_new, p_len=p_len),
        grid=(bn,),
        in_specs=[qspec, qspec, qspec, cspec, cspec,
                  pl.BlockSpec((1, FOX_HEADS, n_keys), lambda b: (b, 0, 0))],
        out_specs=qspec,
        out_shape=jax.ShapeDtypeStruct((bn, t_new, FOX_DIM), BF16),
        compiler_params=_params("parallel"),
        name="fox_sample",
    )(q, kn, vn, kc, vc, lf_all)


def _gmlp_kernel(x_ref, win_ref, bin_ref, ng_ref, nb_ref, ws_ref, bst_ref, wout_ref,
                 g_ref, b_ref, y_ref, *rest, tm, alpha, emit_vn, n_sub):
    if emit_vn:
        vn_ref, u_scr, vnb_scr, gated_scr = rest
    else:
        u_scr, vnb_scr, gated_scr = rest
    row = lax.broadcasted_iota(jnp.int32, (GMLP_CHUNK, GMLP_CHUNK), 0)
    col = lax.broadcasted_iota(jnp.int32, (GMLP_CHUNK, GMLP_CHUNK), 1)
    gd = GMLP_DIM // GMLP_GROUPS
    rows_per = tm // n_sub
    for sb in range(n_sub):
        rows = slice(sb * rows_per, (sb + 1) * rows_per)
        x = x_ref[rows, :]
        xb = x.astype(BF16)
        v = _gelu(_dot(xb, win_ref[:, GMLP_DIM:]) + bin_ref[:, GMLP_DIM:])
        vn = _layer_norm(v, ng_ref[...], nb_ref[...])
        if emit_vn:
            vn_ref[rows, :] = vn
        vnb_scr[rows, :] = vn.astype(BF16)
        u_scr[rows, :] = _gelu(_dot(xb, win_ref[:, 0:GMLP_DIM]) + bin_ref[:, 0:GMLP_DIM])

        chunks = range(sb * rows_per // GMLP_CHUNK, (sb + 1) * rows_per // GMLP_CHUNK)
        for g in range(GMLP_GROUPS):
            wm = jnp.where(col <= row, ws_ref[g], 0.0).astype(BF16)
            bias = bst_ref[:, g:g + 1]
            cs = slice(g * gd, (g + 1) * gd)
            for r in range(chunks.start, chunks.stop, 2):
                rss = [slice(rr * GMLP_CHUNK, (rr + 1) * GMLP_CHUNK) for rr in range(r, min(r + 2, chunks.stop))]
                mixed = _dot(wm, jnp.concatenate([vnb_scr[rs, cs] for rs in rss], axis=1)) + bias
                for n, rs in enumerate(rss):
                    gated_scr[rs, cs] = (u_scr[rs, cs] * mixed[:, n * gd:(n + 1) * gd]).astype(BF16)
        out = _dot(gated_scr[rows, :], wout_ref[...])
        y_ref[rows, :] = _layer_norm(alpha * x + out, g_ref[...], b_ref[...])


def _gmlp_mixer(x, layer, win, bin_, ng, nb, ws, bst, wout, g, b, alpha, tm, emit_vn):
    n = x.shape[0]
    tok = pl.BlockSpec((tm, D_MODEL), lambda i: (i, 0))
    out_shape = [jax.ShapeDtypeStruct((n, D_MODEL), F32)]
    out_specs = [tok]
    if emit_vn:
        out_shape.append(jax.ShapeDtypeStruct((n, GMLP_DIM), F32))
        out_specs.append(tok)
    res = pl.pallas_call(
        functools.partial(_gmlp_kernel, tm=tm, alpha=alpha, emit_vn=emit_vn, n_sub=max(1, tm // TOKEN_TILE)),
        grid=(n // tm,),
        in_specs=[tok, _resident(win.shape, layer), _resident(bin_.shape), _resident(ng.shape),
                  _resident(nb.shape), _resident(ws.shape, layer), _resident(bst.shape),
                  _resident(wout.shape, layer), _resident(g.shape), _resident(b.shape)],
        out_specs=tuple(out_specs),
        out_shape=tuple(out_shape),
        scratch_shapes=[pltpu.VMEM((tm, GMLP_DIM), F32), pltpu.VMEM((tm, GMLP_DIM), BF16),
                        pltpu.VMEM((tm, GMLP_DIM), BF16)],
        compiler_params=_params("parallel"),
        name="gmlp_mixer",
    )(x, win, bin_, ng, nb, ws, bst, wout, g, b)
    return res if emit_vn else (res[0], None)


def _mem_attn_kernel(x_ref, *refs, alpha, fuse_even, project_kv, n_sub):
    if fuse_even:
        att_ref, gc_ref, wmix_ref, g0_ref, b0_ref = refs[:5]
        refs = refs[5:]
    if project_kv:
        (mem_ref, wk_ref, wv_ref, wq_ref, wo_ref,
         y_ref, mk_ref, mv_ref, mkb_ref, mvb_ref) = refs
    else:
        mk_ref, mv_ref, wq_ref, wo_ref, y_ref, mkb_ref, mvb_ref = refs

    @pl.when(pl.program_id(1) == 0)
    def _():
        if project_kv:
            mb = mem_ref[0].astype(BF16)
            k = _dot(mb, wk_ref[...])
            v = _dot(mb, wv_ref[...])
        for h in range(MEM_HEADS):
            hs = slice(h * MEM_HEAD_DIM, (h + 1) * MEM_HEAD_DIM)
            if project_kv:
                mk_ref[0, :, h, :] = k[:, hs]
                mv_ref[0, :, h, :] = v[:, hs]
                mkb_ref[h] = k[:, hs].astype(BF16)
                mvb_ref[h] = v[:, hs].astype(BF16)
            else:
                mkb_ref[h] = mk_ref[0, 0, :, h, :].astype(BF16)
                mvb_ref[h] = mv_ref[0, 0, :, h, :].astype(BF16)

    rows_per = x_ref.shape[1] // n_sub
    for sb in range(n_sub):
        rows = slice(sb * rows_per, (sb + 1) * rows_per)
        x = x_ref[0, rows, :]
        if fuse_even:
            mix = _dot(att_ref[0, rows, :], wmix_ref[0:FOX_DIM, :]) + _dot(gc_ref[0, rows, :], wmix_ref[FOX_DIM:, :])
            x = _layer_norm(alpha * x + mix, g0_ref[...], b0_ref[...])
        q = _dot(x.astype(BF16), wq_ref[...]) * (LOG2E * MEM_HEAD_DIM ** -0.5)
        qb = q.astype(BF16)
        heads = []
        for h in range(MEM_HEADS):
            hs = slice(h * MEM_HEAD_DIM, (h + 1) * MEM_HEAD_DIM)
            s = _dot_nt(qb[:, hs], mkb_ref[h])
            m = jnp.max(s, axis=-1, keepdims=True)
            p = jnp.exp2(s - m)
            l = jnp.sum(p, axis=-1, keepdims=True)
            heads.append((_dot(p.astype(BF16), mvb_ref[h]) / l).astype(BF16))
        o = jnp.concatenate(heads, axis=-1)
        y_ref[0, rows, :] = alpha * x + _dot(o, wo_ref[...])


def _mem_attn(x, kv, layer, wq, wo, alpha, tm, even=None):
    bn, tlen, _ = x.shape
    project_kv = len(kv) == 3
    mlen = kv[0].shape[1] if project_kv else kv[0].shape[2]
    tok = lambda w: pl.BlockSpec((1, tm, w), lambda bb, t: (bb, t, 0))
    if project_kv:
        kv_specs = [pl.BlockSpec((1, mlen, D_MODEL), lambda bb, t: (bb, 0, 0)),
                    _resident(kv[1].shape, layer), _resident(kv[2].shape, layer)]
    else:
        kv_specs = [pl.BlockSpec((1, 1, mlen, MEM_HEADS, MEM_HEAD_DIM), lambda bb, t: (layer, bb, 0, 0, 0))] * 2
    pre_args, pre_specs = [], []
    if even is not None:
        att, gc, wmix, e, g0, b0 = even
        pre_args = [att, gc, wmix, g0, b0]
        pre_specs = [tok(FOX_DIM), tok(CONV_DIM), _resident(wmix.shape, e), _resident(g0.shape),
                     _resident(b0.shape)]
    out_specs = [tok(D_MODEL)]
    out_shape = [jax.ShapeDtypeStruct((bn, tlen, D_MODEL), F32)]
    if project_kv:
        out_specs += [pl.BlockSpec((1, mlen, MEM_HEADS, MEM_HEAD_DIM), lambda bb, t: (bb, 0, 0, 0))] * 2
        out_shape += [jax.ShapeDtypeStruct((bn, mlen, MEM_HEADS, MEM_HEAD_DIM), F32)] * 2
    res = pl.pallas_call(
        functools.partial(_mem_attn_kernel, alpha=alpha, fuse_even=even is not None,
                          project_kv=project_kv, n_sub=max(1, tm // TOKEN_TILE)),
        grid=(bn, tlen // tm),
        in_specs=[tok(D_MODEL)] + pre_specs + kv_specs + [_resident(wq.shape, layer), _resident(wo.shape, layer)],
        out_specs=tuple(out_specs),
        out_shape=tuple(out_shape),
        scratch_shapes=[pltpu.VMEM((MEM_HEADS, mlen, MEM_HEAD_DIM), BF16),
                        pltpu.VMEM((MEM_HEADS, mlen, MEM_HEAD_DIM), BF16)],
        compiler_params=_params("parallel", "arbitrary"),
        name="mem_attn",
    )(x, *pre_args, *kv, wq, wo)
    return res if project_kv else res[0]


def _ffn_kernel(x_ref, gin_ref, bin_ref, wg_ref, wu_ref, wd_ref, g_ref, b_ref, y_ref, h_scr, *, alpha, n_sub):
    d_ff = wg_ref.shape[1]
    rows_per = x_ref.shape[0] // n_sub
    for s in range(n_sub):
        rows = slice(s * rows_per, (s + 1) * rows_per)
        x = _layer_norm(x_ref[rows, :], gin_ref[...], bin_ref[...])
        xb = x.astype(BF16)
        for c in range(d_ff // FFN_CHUNK):
            cols = slice(c * FFN_CHUNK, (c + 1) * FFN_CHUNK)
            gate = _dot(xb, wg_ref[:, cols])
            up = _dot(xb, wu_ref[:, cols])
            h_scr[rows, cols] = (gate * jax.nn.sigmoid(gate) * up).astype(BF16)
        y = alpha * x + _dot(h_scr[rows, :], wd_ref[...])
        y_ref[rows, :] = _layer_norm(y, g_ref[...], b_ref[...])


def _ffn(x, g_in, b_in, layer, wg, wu, wd, g, b, alpha, tm):
    n = x.shape[0]
    d_ff = wg.shape[-1]
    assert d_ff % FFN_CHUNK == 0
    tok = pl.BlockSpec((tm, D_MODEL), lambda i: (i, 0))
    return pl.pallas_call(
        functools.partial(_ffn_kernel, alpha=alpha, n_sub=max(1, tm // TOKEN_TILE)),
        grid=(n // tm,),
        in_specs=[tok, _resident(g_in.shape), _resident(b_in.shape), _resident(wg.shape, layer),
                  _resident(wu.shape, layer), _resident(wd.shape, layer), _resident(g.shape), _resident(b.shape)],
        out_specs=tok,
        out_shape=jax.ShapeDtypeStruct((n, D_MODEL), F32),
        scratch_shapes=[pltpu.VMEM((tm, d_ff), BF16)],
        compiler_params=_params("parallel"),
        name="swiglu_ffn",
    )(x, g_in, b_in, wg, wu, wd, g, b)


def kernel(x_prompt, x_sample, cache_fox_k, cache_fox_v, cache_fox_logf, state_conv, cache_mem_k, cache_mem_v, mem_prompt, w_in_even, b_forget, conv_w, w_out_even, w_in_odd, b_in_odd, gmlp_norm_g, gmlp_norm_b, gmlp_w_s, gmlp_b_s, w_out_odd, mem_w_q, mem_w_k, mem_w_v, mem_w_o, ffn_w_gate, ffn_w_up, ffn_w_down, ln_g, ln_b):
    depth = ln_g.shape[0]
    alpha = float((2 * depth) ** 0.25)
    bp, s_len, d = x_prompt.shape
    bs, t_new, _ = x_sample.shape
    p_len = cache_fox_k.shape[2]
    row2 = lambda a: a.reshape(1, -1)

    (w_out_even_b, w_in_odd_b, w_out_odd_b, mem_wq_b, mem_wo_b, mem_wk_b, mem_wv_b,
     ffn_wg_b, ffn_wu_b, ffn_wd_b) = _to_bf16([w_out_even, w_in_odd, w_out_odd, mem_w_q, mem_w_o, mem_w_k,
                                               mem_w_v, ffn_w_gate, ffn_w_up, ffn_w_down])

    yp, ys = x_prompt, x_sample
    fk_p, fv_p, fl_p, cs_p, mk_p, mv_p = [], [], [], [], [], []
    fk_s, fv_s, fl_s, cs_s, gv_s = [], [], [], [], []
    for layer in range(depth):
        lg, lb = ln_g[layer], ln_b[layer]
        even_p = even_s = None
        if layer % 2 == 0:
            e = layer // 2
            w_in = w_in_even[e].astype(BF16)
            qkv_w = w_in[:, 0:3 * FOX_DIM]
            f_wt = w_in[:, 3 * FOX_DIM:3 * FOX_DIM + FOX_HEADS].T
            c_w = w_in[:, 3 * FOX_DIM + FOX_HEADS:]
            f_b = b_forget[e].reshape(FOX_HEADS, 1)
            cw = conv_w[e]

            st0 = jnp.zeros((bp, SUBLANES, CONV_DIM), F32)
            q, kf, vf, kb, vb, lf, gc, cst, c = _even_inproj(yp, qkv_w, f_wt, c_w, f_b, cw, st0,
                                                             TOKEN_TILE, True)
            att = _fox_prompt(q, kb, vb, c, FOX_TILE)
            even_p = (att, gc, w_out_even_b, e, row2(lg[0]), row2(lb[0]))
            fk_p.append(kf.reshape(bp, s_len, FOX_HEADS, FOX_HEAD_DIM))
            fv_p.append(vf.reshape(bp, s_len, FOX_HEADS, FOX_HEAD_DIM))
            fl_p.append(jnp.swapaxes(lf, 1, 2))
            cs_p.append(cst)

            st = jnp.pad(state_conv[e], ((0, 0), (SUBLANES - (CONV_WIDTH - 1), 0), (0, 0)))
            q, kf, vf, kb, vb, lf, gc, cst = _even_inproj(ys, qkv_w, f_wt, c_w, f_b, cw, st, t_new, False)
            pad = -(p_len + t_new) % LANES
            lf_all = jnp.concatenate([jnp.swapaxes(cache_fox_logf[e], 1, 2), lf,
                                      jnp.zeros((bs, FOX_HEADS, pad), F32)], axis=2)
            att = _fox_sample(q, kb, vb, cache_fox_k[e].reshape(bs, p_len, FOX_DIM),
                              cache_fox_v[e].reshape(bs, p_len, FOX_DIM), lf_all)
            even_s = (att, gc, w_out_even_b, e, row2(lg[0]), row2(lb[0]))
            fk_s.append(kf.reshape(bs, t_new, FOX_HEADS, FOX_HEAD_DIM))
            fv_s.append(vf.reshape(bs, t_new, FOX_HEADS, FOX_HEAD_DIM))
            fl_s.append(jnp.swapaxes(lf, 1, 2))
            cs_s.append(cst)
        else:
            o = layer // 2
            args = (o, w_in_odd_b, row2(b_in_odd[o]), row2(gmlp_norm_g[o]), row2(gmlp_norm_b[o]),
                    gmlp_w_s, gmlp_b_s[o].T, w_out_odd_b, row2(lg[0]), row2(lb[0]), alpha)
            yp, _ = _gmlp_mixer(yp.reshape(-1, d), *args, 2 * TOKEN_TILE, False)
            yp = yp.reshape(bp, s_len, d)
            ys_pad = jnp.pad(ys, ((0, 0), (0, GMLP_CHUNK - t_new), (0, 0))).reshape(-1, d)
            ys_pad, vn = _gmlp_mixer(ys_pad, *args, TOKEN_TILE, True)
            ys = ys_pad.reshape(bs, GMLP_CHUNK, d)[:, :t_new]
            gv_s.append(vn.reshape(bs, GMLP_CHUNK, GMLP_DIM)[:, :t_new])

        yp, mk, mv = _mem_attn(yp, (mem_prompt, mem_wk_b, mem_wv_b), layer, mem_wq_b, mem_wo_b,
                               alpha, 2 * TOKEN_TILE, even_p)
        mk_p.append(mk)
        mv_p.append(mv)
        ys = _mem_attn(ys, (cache_mem_k, cache_mem_v), layer, mem_wq_b, mem_wo_b, alpha, t_new, even_s)

        ffn_args = (row2(lg[1]), row2(lb[1]), layer, ffn_wg_b, ffn_wu_b, ffn_wd_b, row2(lg[2]), row2(lb[2]),
                    alpha)
        yp = _ffn(yp.reshape(-1, d), *ffn_args, FFN_ROWS_PER_STEP).reshape(bp, s_len, d)
        ys = _ffn(ys.reshape(-1, d), *ffn_args, bs * t_new).reshape(bs, t_new, d)

    return (yp, ys, jnp.stack(fk_p), jnp.stack(fv_p), jnp.stack(fl_p), jnp.stack(cs_p),
            jnp.stack(mk_p), jnp.stack(mv_p),
            jnp.stack(fk_s), jnp.stack(fv_s), jnp.stack(fl_s), jnp.stack(cs_s), jnp.stack(gv_s))
```

```python
import functools

import jax
import jax.numpy as jnp
from jax import lax
from jax.experimental import pallas as pl
from jax.experimental.pallas import tpu as pltpu

F32 = jnp.float32
BF16 = jnp.bfloat16

D_MODEL = 1024
FOX_HEADS = 8
FOX_HEAD_DIM = 64
FOX_DIM = FOX_HEADS * FOX_HEAD_DIM
CONV_DIM = 512
CONV_WIDTH = 3
GMLP_CHUNK = 128
GMLP_GROUPS = 8
GMLP_DIM = D_MODEL
MEM_HEADS = 4
MEM_HEAD_DIM = 128
MEM_DIM = MEM_HEADS * MEM_HEAD_DIM
LN_EPS = 1e-5

LANES = 128
SUBLANES = 8
VMEM_LIMIT_BYTES = 56 * 1024 * 1024
TOKEN_TILE = 512
FOX_TILE = 512
FOX_KEY_BLOCKS = 2
FOX_QUERY_BLOCKS = 1
FFN_CHUNK = 256
FFN_ROWS_PER_STEP = 2 * TOKEN_TILE
FFN_BLOCK_ROWS = 256
LOG2E = 1.4426950408889634
CAST_BLOCK_BYTES = 1024 * 1024


def _params(*semantics):
    return pltpu.CompilerParams(dimension_semantics=semantics,
                                vmem_limit_bytes=VMEM_LIMIT_BYTES)


def _resident(shape, layer=None):
    if layer is None:
        nd = len(shape)
        return pl.BlockSpec(shape, lambda *_: (0,) * nd, pipeline_mode=pl.Buffered(1))
    nd = len(shape) - 1
    return pl.BlockSpec((None,) + tuple(shape[1:]), lambda *_: (layer,) + (0,) * nd,
                        pipeline_mode=pl.Buffered(1))


def _dot(a, b):
    return jnp.dot(a, b, preferred_element_type=F32)


def _dot_nt(a, b):
    return lax.dot_general(a, b, (((1,), (1,)), ((), ())), preferred_element_type=F32)


def _layer_norm(y, g, b):
    mu = jnp.mean(y, axis=-1, keepdims=True)
    d = y - mu
    var = jnp.mean(d * d, axis=-1, keepdims=True)
    return d * lax.rsqrt(var + LN_EPS) * g + b


def _gelu(z):
    return 0.5 * z * (1.0 + lax.erf(z * (2.0 ** -0.5)))


def _cast_kernel(*refs, n_blocks):
    n = len(n_blocks)
    i = pl.program_id(0)
    for k in range(n):
        @pl.when(i < n_blocks[k])
        def _(k=k):
            refs[n + k][...] = refs[k][...].astype(BF16)


def _to_bf16(ws):
    flat, specs, n_blocks = [], [], []
    for w in ws:
        n = w.shape[-1]
        rows = w.size // n
        blk = rows
        while blk * n * w.dtype.itemsize > CAST_BLOCK_BYTES and blk % (4 * SUBLANES) == 0:
            blk //= 2
        nb = rows // blk
        flat.append(w.reshape(rows, n))
        specs.append(pl.BlockSpec((blk, n), lambda i, nb=nb: (jnp.minimum(i, nb - 1), 0)))
        n_blocks.append(nb)
    outs = pl.pallas_call(
        functools.partial(_cast_kernel, n_blocks=tuple(n_blocks)),
        grid=(max(n_blocks),), in_specs=specs, out_specs=tuple(specs),
        out_shape=tuple(jax.ShapeDtypeStruct(f.shape, BF16) for f in flat),
        compiler_params=_params("arbitrary"), name="to_bf16",
    )(*flat)
    return [o.reshape(w.shape) for o, w in zip(outs, ws)]


def _lane_prefix_sum(x, carry):
    rows, n = x.shape
    lane = lax.broadcasted_iota(jnp.int32, (rows, LANES), 1)
    segs = []
    for j in range(n // LANES):
        seg = x[:, j * LANES:(j + 1) * LANES]
        step = 1
        while step < LANES:
            seg = seg + jnp.where(lane >= step, pltpu.roll(seg, step, 1), 0.0)
            step *= 2
        segs.append(seg)
    outs = []
    for seg in segs:
        seg = seg + carry
        outs.append(seg)
        carry = seg[:, LANES - 1:LANES]
    return jnp.concatenate(outs, axis=1), carry


def _even_inproj_kernel(x_ref, wqkv_ref, wft_ref, wc_ref, bf_ref, cw_ref, st_ref,
                        q_ref, kf_ref, vf_ref, kb_ref, vb_ref, lf_ref, gc_ref, cs_ref, *rest,
                        tm, nt, emit_c):
    if emit_c:
        c_ref, prev_ref, ccarry_ref = rest
    else:
        (prev_ref,) = rest
    t = pl.program_id(1)

    @pl.when(t == 0)
    def _():
        prev_ref[...] = st_ref[0]
        if emit_c:
            ccarry_ref[...] = jnp.zeros_like(ccarry_ref)

    n_sub = max(1, tm // TOKEN_TILE)
    tb = tm // n_sub
    prev = prev_ref[...]
    carry = ccarry_ref[...] if emit_c else None
    cw = cw_ref[...]
    row = lax.broadcasted_iota(jnp.int32, (tb, CONV_DIM), 0)
    for sb in range(n_sub):
        rows = slice(sb * tb, (sb + 1) * tb)
        xb = x_ref[0, rows, :].astype(BF16)
        lf = jax.nn.log_sigmoid(_dot_nt(wft_ref[...], xb) + bf_ref[...])
        lf_ref[0, :, rows] = lf
        if emit_c:
            c, carry = _lane_prefix_sum(lf, carry)
            c_ref[0, :, rows] = c

        h = _dot(xb, wc_ref[:, 0:CONV_DIM])
        bg = _dot(xb, wc_ref[:, CONV_DIM:2 * CONV_DIM])
        cg = _dot(xb, wc_ref[:, 2 * CONV_DIM:3 * CONV_DIM])
        pre = cg * h

        p1 = jnp.where(row == 0, prev[7:8, :], pltpu.roll(pre, 1, 0))
        p2 = jnp.where(row == 0, prev[6:7, :],
                       jnp.where(row == 1, prev[7:8, :], pltpu.roll(pre, 2, 0)))
        conv = cw[0:1, :] * p2 + cw[1:2, :] * p1 + cw[2:3, :] * pre
        gc_ref[0, rows, :] = (bg * conv).astype(BF16)
        prev = pre[tb - SUBLANES:tb, :]

        q = _dot(xb, wqkv_ref[:, 0:FOX_DIM])
        q_ref[0, rows, :] = (q * (LOG2E * FOX_HEAD_DIM ** -0.5)).astype(BF16)
        k = _dot(xb, wqkv_ref[:, FOX_DIM:2 * FOX_DIM])
        kf_ref[0, rows, :] = k
        kb_ref[0, rows, :] = k.astype(BF16)
        v = _dot(xb, wqkv_ref[:, 2 * FOX_DIM:3 * FOX_DIM])
        vf_ref[0, rows, :] = v
        vb_ref[0, rows, :] = v.astype(BF16)
    prev_ref[...] = prev
    if emit_c:
        ccarry_ref[...] = carry

    @pl.when(t == nt - 1)
    def _():
        cs_ref[0] = prev[SUBLANES - (CONV_WIDTH - 1):SUBLANES, :]


def _even_inproj(x, wqkv, wft, wc, bf, cw, state8, tm, emit_c):
    bn, tlen, _ = x.shape
    nt = tlen // tm
    tok = lambda w: pl.BlockSpec((1, tm, w), lambda b, t: (b, t, 0))
    tlane = pl.BlockSpec((1, FOX_HEADS, tm), lambda b, t: (b, 0, t))
    out_shape = [
        jax.ShapeDtypeStruct((bn, tlen, FOX_DIM), BF16),
        jax.ShapeDtypeStruct((bn, tlen, FOX_DIM), F32),
        jax.ShapeDtypeStruct((bn, tlen, FOX_DIM), F32),
        jax.ShapeDtypeStruct((bn, tlen, FOX_DIM), BF16),
        jax.ShapeDtypeStruct((bn, tlen, FOX_DIM), BF16),
        jax.ShapeDtypeStruct((bn, FOX_HEADS, tlen), F32),
        jax.ShapeDtypeStruct((bn, tlen, CONV_DIM), BF16),
        jax.ShapeDtypeStruct((bn, CONV_WIDTH - 1, CONV_DIM), F32),
    ]
    out_specs = [tok(FOX_DIM), tok(FOX_DIM), tok(FOX_DIM), tok(FOX_DIM), tok(FOX_DIM), tlane,
                 tok(CONV_DIM), pl.BlockSpec((1, CONV_WIDTH - 1, CONV_DIM), lambda b, t: (b, 0, 0))]
    scratch = [pltpu.VMEM((SUBLANES, CONV_DIM), F32)]
    if emit_c:
        out_shape.append(jax.ShapeDtypeStruct((bn, FOX_HEADS, tlen), F32))
        out_specs.append(tlane)
        scratch.append(pltpu.VMEM((FOX_HEADS, 1), F32))
    return pl.pallas_call(
        functools.partial(_even_inproj_kernel, tm=tm, nt=nt, emit_c=emit_c),
        grid=(bn, nt),
        in_specs=[tok(D_MODEL), _resident(wqkv.shape), _resident(wft.shape),
                  _resident(wc.shape), _resident(bf.shape), _resident(cw.shape),
                  pl.BlockSpec((1, SUBLANES, CONV_DIM), lambda b, t: (b, 0, 0))],
        out_specs=tuple(out_specs),
        out_shape=tuple(out_shape),
        scratch_shapes=scratch,
        compiler_params=_params("parallel", "arbitrary"),
        name="even_inproj",
    )(x, wqkv, wft, wc, bf, cw, state8)


def _head_mask(shape, hh):
    lane = lax.broadcasted_iota(jnp.int32, shape, 1)
    return (lane >= FOX_HEAD_DIM) if hh == 1 else (lane < FOX_HEAD_DIM)


def _fox_prompt_kernel(q_ref, k_ref, v_ref, c_ref, o_ref, vt_ref, cb_ref, *, tq, nq):
    hp = pl.program_id(1)
    i = pl.program_id(2)
    half = lax.broadcasted_iota(jnp.int32, (LANES, tq), 0) < FOX_HEAD_DIM

    @pl.when(i == 0)
    def _():
        for j in range(nq):
            rows = slice(j * tq, (j + 1) * tq)
            vt = v_ref[0, rows, :].astype(F32).T
            for hh in range(2):
                own = half if hh == 0 else jnp.logical_not(half)
                vt_ref[hh, j] = jnp.where(own, vt, 1.0).astype(BF16)
                c_j = c_ref[0, (hp * 2 + hh) * nq + j]
                rel = (c_j[:, 0:1] - c_j) * LOG2E
                cb_ref[hh, rows, :] = jnp.broadcast_to(rel, (LANES, tq)).T

    q = q_ref[0]
    qh = [jnp.where(_head_mask(q.shape, hh), q, jnp.zeros_like(q)) for hh in range(2)]
    wk = tq // FOX_KEY_BLOCKS
    wq = tq // FOX_QUERY_BLOCKS
    key = lax.broadcasted_iota(jnp.int32, (wk, wq), 0)
    qry = lax.broadcasted_iota(jnp.int32, (wk, wq), 1)
    streams = [(hh, qb) for hh in range(2) for qb in range(FOX_QUERY_BLOCKS)]

    def c_start(j, hh):
        return c_ref[0, (hp * 2 + hh) * nq + j][:, 0:1]

    def scores(j, kb, hh, qb):
        rows = pl.ds(pl.multiple_of(j * tq + kb * wk, wk), wk)
        st = _dot_nt(k_ref[0, rows, :], qh[hh][qb * wq:(qb + 1) * wq, :])
        return st + jnp.tile(cb_ref[hh, rows, :], (1, wq // LANES))

    def chain(tiles, carry, diagonal=False):
        blocks = [(i, kb, True) for kb in range(FOX_KEY_BLOCKS)] if diagonal else []
        blocks += [(j, kb, False) for j in tiles for kb in range(FOX_KEY_BLOCKS)]
        live = lambda kb, qb, diag: not diag or kb * wk < (qb + 1) * wq
        sts = [[scores(j, kb, hh, qb) if live(kb, qb, diag) else None for hh, qb in streams]
               for j, kb, diag in blocks]
        carry = [None] * len(streams) if carry is None else list(carry)
        for (j, kb, diag), st_b in zip(blocks, sts):
            for n, (hh, qb) in enumerate(streams):
                st = st_b[n]
                if st is None:
                    continue
                if diag:
                    if (kb + 1) * wk - 1 > qb * wq:
                        st = jnp.where(key + kb * wk <= qry + qb * wq, st, -jnp.inf)
                    top = jnp.max(st, axis=0, keepdims=True)
                else:
                    delta = (c_start(i, hh) - c_start(j, hh)) * LOG2E
                    top = jnp.max(st, axis=0, keepdims=True) + delta
                m_new = top if carry[n] is None else jnp.maximum(carry[n][0], top)
                p = jnp.exp2(st - (m_new if diag else m_new - delta)).astype(BF16)
                pv = _dot(vt_ref[hh, j, :, kb * wk:(kb + 1) * wk], p)
                if carry[n] is None:
                    carry[n] = (m_new, pv)
                else:
                    carry[n] = (m_new, jnp.exp2(carry[n][0] - m_new) * carry[n][1] + pv)
        return tuple(carry)

    carry = lax.cond(i % 2 == 1, lambda: chain([i - 1], None, True), lambda: chain([], None, True))
    carry = lax.fori_loop(0, i // 2, lambda jj, c: chain([2 * jj, 2 * jj + 1], c), carry)
    accs = [jnp.concatenate([carry[hh * FOX_QUERY_BLOCKS + qb][1] for qb in range(FOX_QUERY_BLOCKS)], axis=1)
            for hh in range(2)]
    out0 = accs[0] / accs[0][FOX_HEAD_DIM:FOX_HEAD_DIM + 1, :]
    out1 = accs[1] / accs[1][0:1, :]
    o_ref[0] = jnp.where(half, out0, out1).T.astype(o_ref.dtype)


def _fox_prompt(q, k, v, c, tq):
    bn, s_len, _ = q.shape
    nq = s_len // tq
    c4 = c.reshape(bn, FOX_HEADS * nq, 1, tq)
    qspec = pl.BlockSpec((1, tq, LANES), lambda b, h, i: (b, i, h))
    kspec = pl.BlockSpec((1, s_len, LANES), lambda b, h, i: (b, 0, h))
    return pl.pallas_call(
        functools.partial(_fox_prompt_kernel, tq=tq, nq=nq),
        grid=(bn, FOX_DIM // LANES, nq),
        in_specs=[qspec, kspec, kspec,
                  pl.BlockSpec((1, FOX_HEADS * nq, 1, tq), lambda b, h, i: (b, 0, 0, 0))],
        out_specs=qspec,
        out_shape=jax.ShapeDtypeStruct((bn, s_len, FOX_DIM), BF16),
        scratch_shapes=[pltpu.VMEM((2, nq, LANES, tq), BF16),
                        pltpu.VMEM((2, s_len, LANES), F32)],
        compiler_params=_params("parallel", "parallel", "arbitrary"),
        name="fox_prompt",
    )(q, k, v, c4)


def _fox_sample_kernel(q_ref, kn_ref, vn_ref, kc_ref, vc_ref, lf_ref, o_ref, *, t_new, p_len):
    n_new = lf_ref.shape[-1] - p_len
    zeros = jnp.zeros((n_new - t_new, LANES), BF16)
    row = lax.broadcasted_iota(jnp.int32, (t_new, n_new), 0)
    col = lax.broadcasted_iota(jnp.int32, (t_new, n_new), 1)
    c_all, _ = _lane_prefix_sum(lf_ref[0], jnp.zeros((FOX_HEADS, 1), F32))
    for hp in range(FOX_DIM // LANES):
        lanes = slice(hp * LANES, (hp + 1) * LANES)
        q = q_ref[0, :, lanes]
        kn = jnp.concatenate([kn_ref[0, :, lanes], zeros], axis=0)
        vn = jnp.concatenate([vn_ref[0, :, lanes], zeros], axis=0)
        kc = kc_ref[0, :, lanes].astype(BF16)
        vc = vc_ref[0, :, lanes].astype(BF16)
        outs = []
        for hh in range(2):
            head = hp * 2 + hh
            qh = jnp.where(_head_mask(q.shape, hh), q, jnp.zeros_like(q))
            c_row = c_all[head:head + 1, :]
            bias = (c_row[:, p_len - 1:p_len] - c_row) * LOG2E
            s_c = _dot_nt(qh, kc) + bias[:, 0:p_len]
            s_n = jnp.where(col <= row, _dot_nt(qh, kn) + bias[:, p_len:], -jnp.inf)
            m = jnp.maximum(jnp.max(s_c, axis=-1, keepdims=True), jnp.max(s_n, axis=-1, keepdims=True))
            p_c = jnp.exp2(s_c - m)
            p_n = jnp.exp2(s_n - m)
            l = jnp.sum(p_c, axis=-1, keepdims=True) + jnp.sum(p_n, axis=-1, keepdims=True)
            outs.append((_dot(p_c.astype(BF16), vc) + _dot(p_n.astype(BF16), vn)) / l)
        o_ref[0, :, lanes] = jnp.where(_head_mask(outs[0].shape, 0), outs[0], outs[1]).astype(o_ref.dtype)


def _fox_sample(q, kn, vn, kc, vc, lf_all):
    bn, t_new, _ = q.shape
    p_len = kc.shape[1]
    n_keys = lf_all.shape[-1]
    qspec = pl.BlockSpec((1, t_new, FOX_DIM), lambda b: (b, 0, 0))
    cspec = pl.BlockSpec((1, p_len, FOX_DIM), lambda b: (b, 0, 0))
    return pl.pallas_call(
        functools.partial(_fox_sample_kernel, t_new=t_new, p_len=p_len),
        grid=(bn,),
        in_specs=[qspec, qspec, qspec, cspec, cspec,
                  pl.BlockSpec((1, FOX_HEADS, n_keys), lambda b: (b, 0, 0))],
        out_specs=qspec,
        out_shape=jax.ShapeDtypeStruct((bn, t_new, FOX_DIM), BF16),
        compiler_params=_params("parallel"),
        name="fox_sample",
    )(q, kn, vn, kc, vc, lf_all)


def _gmlp_kernel(x_ref, win_ref, bin_ref, ng_ref, nb_ref, ws_ref, bst_ref, wout_ref,
                 g_ref, b_ref, y_ref, *rest, tm, alpha, emit_vn, n_sub):
    if emit_vn:
        vn_ref, u_scr, vnb_scr, gated_scr = rest
    else:
        u_scr, vnb_scr, gated_scr = rest
    row = lax.broadcasted_iota(jnp.int32, (GMLP_CHUNK, GMLP_CHUNK), 0)
    col = lax.broadcasted_iota(jnp.int32, (GMLP_CHUNK, GMLP_CHUNK), 1)
    gd = GMLP_DIM // GMLP_GROUPS
    rows_per = tm // n_sub
    for sb in range(n_sub):
        rows = slice(sb * rows_per, (sb + 1) * rows_per)
        x = x_ref[rows, :]
        xb = x.astype(BF16)
        v = _gelu(_dot(xb, win_ref[:, GMLP_DIM:]) + bin_ref[:, GMLP_DIM:])
        vn = _layer_norm(v, ng_ref[...], nb_ref[...])
        if emit_vn:
            vn_ref[rows, :] = vn
        vnb_scr[rows, :] = vn.astype(BF16)
        u_scr[rows, :] = _gelu(_dot(xb, win_ref[:, 0:GMLP_DIM]) + bin_ref[:, 0:GMLP_DIM])

        chunks = range(sb * rows_per // GMLP_CHUNK, (sb + 1) * rows_per // GMLP_CHUNK)
        for g in range(GMLP_GROUPS):
            wm = jnp.where(col <= row, ws_ref[g], 0.0).astype(BF16)
            bias = bst_ref[:, g:g + 1]
            cs = slice(g * gd, (g + 1) * gd)
            for r in range(chunks.start, chunks.stop, 2):
                rss = [slice(rr * GMLP_CHUNK, (rr + 1) * GMLP_CHUNK) for rr in range(r, min(r + 2, chunks.stop))]
                mixed = _dot(wm, jnp.concatenate([vnb_scr[rs, cs] for rs in rss], axis=1)) + bias
                for n, rs in enumerate(rss):
                    gated_scr[rs, cs] = (u_scr[rs, cs] * mixed[:, n * gd:(n + 1) * gd]).astype(BF16)
        out = _dot(gated_scr[rows, :], wout_ref[...])
        y_ref[rows, :] = _layer_norm(alpha * x + out, g_ref[...], b_ref[...])


def _gmlp_mixer(x, layer, win, bin_, ng, nb, ws, bst, wout, g, b, alpha, tm, emit_vn):
    n = x.shape[0]
    tok = pl.BlockSpec((tm, D_MODEL), lambda i: (i, 0))
    out_shape = [jax.ShapeDtypeStruct((n, D_MODEL), F32)]
    out_specs = [tok]
    if emit_vn:
        out_shape.append(jax.ShapeDtypeStruct((n, GMLP_DIM), F32))
        out_specs.append(tok)
    res = pl.pallas_call(
        functools.partial(_gmlp_kernel, tm=tm, alpha=alpha, emit_vn=emit_vn, n_sub=max(1, tm // TOKEN_TILE)),
        grid=(n // tm,),
        in_specs=[tok, _resident(win.shape, layer), _resident(bin_.shape), _resident(ng.shape),
                  _resident(nb.shape), _resident(ws.shape, layer), _resident(bst.shape),
                  _resident(wout.shape, layer), _resident(g.shape), _resident(b.shape)],
        out_specs=tuple(out_specs),
        out_shape=tuple(out_shape),
        scratch_shapes=[pltpu.VMEM((tm, GMLP_DIM), F32), pltpu.VMEM((tm, GMLP_DIM), BF16),
                        pltpu.VMEM((tm, GMLP_DIM), BF16)],
        compiler_params=_params("parallel"),
        name="gmlp_mixer",
    )(x, win, bin_, ng, nb, ws, bst, wout, g, b)
    return res if emit_vn else (res[0], None)


def _mem_attn_kernel(x_ref, *refs, alpha, fuse_even, project_kv, n_sub):
    if fuse_even:
        att_ref, gc_ref, wmix_ref, g0_ref, b0_ref = refs[:5]
        refs = refs[5:]
    if project_kv:
        (mem_ref, wk_ref, wv_ref, wq_ref, wo_ref,
         y_ref, mk_ref, mv_ref, mkb_ref, mvb_ref) = refs
    else:
        mk_ref, mv_ref, wq_ref, wo_ref, y_ref, mkb_ref, mvb_ref = refs

    @pl.when(pl.program_id(1) == 0)
    def _():
        if project_kv:
            mb = mem_ref[0].astype(BF16)
            k = _dot(mb, wk_ref[...])
            v = _dot(mb, wv_ref[...])
        for h in range(MEM_HEADS):
            hs = slice(h * MEM_HEAD_DIM, (h + 1) * MEM_HEAD_DIM)
            if project_kv:
                mk_ref[0, :, h, :] = k[:, hs]
                mv_ref[0, :, h, :] = v[:, hs]
                mkb_ref[h] = k[:, hs].astype(BF16)
                mvb_ref[h] = v[:, hs].astype(BF16)
            else:
                mkb_ref[h] = mk_ref[0, 0, :, h, :].astype(BF16)
                mvb_ref[h] = mv_ref[0, 0, :, h, :].astype(BF16)

    rows_per = x_ref.shape[1] // n_sub
    for sb in range(n_sub):
        rows = slice(sb * rows_per, (sb + 1) * rows_per)
        x = x_ref[0, rows, :]
        if fuse_even:
            mix = _dot(att_ref[0, rows, :], wmix_ref[0:FOX_DIM, :]) + _dot(gc_ref[0, rows, :], wmix_ref[FOX_DIM:, :])
            x = _layer_norm(alpha * x + mix, g0_ref[...], b0_ref[...])
        q = _dot(x.astype(BF16), wq_ref[...]) * (LOG2E * MEM_HEAD_DIM ** -0.5)
        qb = q.astype(BF16)
        heads = []
        for h in range(MEM_HEADS):
            hs = slice(h * MEM_HEAD_DIM, (h + 1) * MEM_HEAD_DIM)
            s = _dot_nt(qb[:, hs], mkb_ref[h])
            m = jnp.max(s, axis=-1, keepdims=True)
            p = jnp.exp2(s - m)
            l = jnp.sum(p, axis=-1, keepdims=True)
            heads.append((_dot(p.astype(BF16), mvb_ref[h]) / l).astype(BF16))
        o = jnp.concatenate(heads, axis=-1)
        y_ref[0, rows, :] = alpha * x + _dot(o, wo_ref[...])


def _mem_attn(x, kv, layer, wq, wo, alpha, tm, even=None):
    bn, tlen, _ = x.shape
    project_kv = len(kv) == 3
    mlen = kv[0].shape[1] if project_kv else kv[0].shape[2]
    tok = lambda w: pl.BlockSpec((1, tm, w), lambda bb, t: (bb, t, 0))
    if project_kv:
        kv_specs = [pl.BlockSpec((1, mlen, D_MODEL), lambda bb, t: (bb, 0, 0)),
                    _resident(kv[1].shape, layer), _resident(kv[2].shape, layer)]
    else:
        kv_specs = [pl.BlockSpec((1, 1, mlen, MEM_HEADS, MEM_HEAD_DIM), lambda bb, t: (layer, bb, 0, 0, 0))] * 2
    pre_args, pre_specs = [], []
    if even is not None:
        att, gc, wmix, e, g0, b0 = even
        pre_args = [att, gc, wmix, g0, b0]
        pre_specs = [tok(FOX_DIM), tok(CONV_DIM), _resident(wmix.shape, e), _resident(g0.shape),
                     _resident(b0.shape)]
    out_specs = [tok(D_MODEL)]
    out_shape = [jax.ShapeDtypeStruct((bn, tlen, D_MODEL), F32)]
    if project_kv:
        out_specs += [pl.BlockSpec((1, mlen, MEM_HEADS, MEM_HEAD_DIM), lambda bb, t: (bb, 0, 0, 0))] * 2
        out_shape += [jax.ShapeDtypeStruct((bn, mlen, MEM_HEADS, MEM_HEAD_DIM), F32)] * 2
    res = pl.pallas_call(
        functools.partial(_mem_attn_kernel, alpha=alpha, fuse_even=even is not None,
                          project_kv=project_kv, n_sub=max(1, tm // TOKEN_TILE)),
        grid=(bn, tlen // tm),
        in_specs=[tok(D_MODEL)] + pre_specs + kv_specs + [_resident(wq.shape, layer), _resident(wo.shape, layer)],
        out_specs=tuple(out_specs),
        out_shape=tuple(out_shape),
        scratch_shapes=[pltpu.VMEM((MEM_HEADS, mlen, MEM_HEAD_DIM), BF16),
                        pltpu.VMEM((MEM_HEADS, mlen, MEM_HEAD_DIM), BF16)],
        compiler_params=_params("parallel", "arbitrary"),
        name="mem_attn",
    )(x, *pre_args, *kv, wq, wo)
    return res if project_kv else res[0]


def _ffn_kernel(x_ref, gin_ref, bin_ref, wg_ref, wu_ref, wd_ref, g_ref, b_ref, y_ref, h_scr, *, alpha, n_sub):
    d_ff = wg_ref.shape[1]
    rows_per = x_ref.shape[0] // n_sub
    for s in range(n_sub):
        rows = slice(s * rows_per, (s + 1) * rows_per)
        x = _layer_norm(x_ref[rows, :], gin_ref[...], bin_ref[...])
        xb = x.astype(BF16)
        for c in range(d_ff // FFN_CHUNK):
            cols = slice(c * FFN_CHUNK, (c + 1) * FFN_CHUNK)
            gate = _dot(xb, wg_ref[:, cols])
            up = _dot(xb, wu_ref[:, cols])
            h_scr[rows, cols] = (gate * jax.nn.sigmoid(gate) * up).astype(BF16)
        y = alpha * x + _dot(h_scr[rows, :], wd_ref[...])
        y_ref[rows, :] = _layer_norm(y, g_ref[...], b_ref[...])


def _ffn(x, g_in, b_in, layer, wg, wu, wd, g, b, alpha, tm):
    n = x.shape[0]
    d_ff = wg.shape[-1]
    assert d_ff % FFN_CHUNK == 0
    tok = pl.BlockSpec((tm, D_MODEL), lambda i: (i, 0))
    return pl.pallas_call(
        functools.partial(_ffn_kernel, alpha=alpha, n_sub=max(1, tm // FFN_BLOCK_ROWS)),
        grid=(n // tm,),
        in_specs=[tok, _resident(g_in.shape), _resident(b_in.shape), _resident(wg.shape, layer),
                  _resident(wu.shape, layer), _resident(wd.shape, layer), _resident(g.shape), _resident(b.shape)],
        out_specs=tok,
        out_shape=jax.ShapeDtypeStruct((n, D_MODEL), F32),
        scratch_shapes=[pltpu.VMEM((tm, d_ff), BF16)],
        compiler_params=_params("parallel"),
        name="swiglu_ffn",
    )(x, g_in, b_in, wg, wu, wd, g, b)


def kernel(x_prompt, x_sample, cache_fox_k, cache_fox_v, cache_fox_logf, state_conv, cache_mem_k, cache_mem_v, mem_prompt, w_in_even, b_forget, conv_w, w_out_even, w_in_odd, b_in_odd, gmlp_norm_g, gmlp_norm_b, gmlp_w_s, gmlp_b_s, w_out_odd, mem_w_q, mem_w_k, mem_w_v, mem_w_o, ffn_w_gate, ffn_w_up, ffn_w_down, ln_g, ln_b):
    depth = ln_g.shape[0]
    alpha = float((2 * depth) ** 0.25)
    bp, s_len, d = x_prompt.shape
    bs, t_new, _ = x_sample.shape
    p_len = cache_fox_k.shape[2]
    row2 = lambda a: a.reshape(1, -1)

    (w_out_even_b, w_in_odd_b, w_out_odd_b, mem_wq_b, mem_wo_b, mem_wk_b, mem_wv_b,
     ffn_wg_b, ffn_wu_b, ffn_wd_b) = _to_bf16([w_out_even, w_in_odd, w_out_odd, mem_w_q, mem_w_o, mem_w_k,
                                               mem_w_v, ffn_w_gate, ffn_w_up, ffn_w_down])

    yp, ys = x_prompt, x_sample
    fk_p, fv_p, fl_p, cs_p, mk_p, mv_p = [], [], [], [], [], []
    fk_s, fv_s, fl_s, cs_s, gv_s = [], [], [], [], []
    for layer in range(depth):
        lg, lb = ln_g[layer], ln_b[layer]
        even_p = even_s = None
        if layer % 2 == 0:
            e = layer // 2
            w_in = w_in_even[e].astype(BF16)
            qkv_w = w_in[:, 0:3 * FOX_DIM]
            f_wt = w_in[:, 3 * FOX_DIM:3 * FOX_DIM + FOX_HEADS].T
            c_w = w_in[:, 3 * FOX_DIM + FOX_HEADS:]
            f_b = b_forget[e].reshape(FOX_HEADS, 1)
            cw = conv_w[e]

            st0 = jnp.zeros((bp, SUBLANES, CONV_DIM), F32)
            q, kf, vf, kb, vb, lf, gc, cst, c = _even_inproj(yp, qkv_w, f_wt, c_w, f_b, cw, st0,
                                                             2 * TOKEN_TILE, True)
            att = _fox_prompt(q, kb, vb, c, FOX_TILE)
            even_p = (att, gc, w_out_even_b, e, row2(lg[0]), row2(lb[0]))
            fk_p.append(kf.reshape(bp, s_len, FOX_HEADS, FOX_HEAD_DIM))
            fv_p.append(vf.reshape(bp, s_len, FOX_HEADS, FOX_HEAD_DIM))
            fl_p.append(jnp.swapaxes(lf, 1, 2))
            cs_p.append(cst)

            st = jnp.pad(state_conv[e], ((0, 0), (SUBLANES - (CONV_WIDTH - 1), 0), (0, 0)))
            q, kf, vf, kb, vb, lf, gc, cst = _even_inproj(ys, qkv_w, f_wt, c_w, f_b, cw, st, t_new, False)
            pad = -(p_len + t_new) % LANES
            lf_all = jnp.concatenate([jnp.swapaxes(cache_fox_logf[e], 1, 2), lf,
                                      jnp.zeros((bs, FOX_HEADS, pad), F32)], axis=2)
            att = _fox_sample(q, kb, vb, cache_fox_k[e].reshape(bs, p_len, FOX_DIM),
                              cache_fox_v[e].reshape(bs, p_len, FOX_DIM), lf_all)
            even_s = (att, gc, w_out_even_b, e, row2(lg[0]), row2(lb[0]))
            fk_s.append(kf.reshape(bs, t_new, FOX_HEADS, FOX_HEAD_DIM))
            fv_s.append(vf.reshape(bs, t_new, FOX_HEADS, FOX_HEAD_DIM))
            fl_s.append(jnp.swapaxes(lf, 1, 2))
            cs_s.append(cst)
        else:
            o = layer // 2
            args = (o, w_in_odd_b, row2(b_in_odd[o]), row2(gmlp_norm_g[o]), row2(gmlp_norm_b[o]),
                    gmlp_w_s, gmlp_b_s[o].T, w_out_odd_b, row2(lg[0]), row2(lb[0]), alpha)
            yp, _ = _gmlp_mixer(yp.reshape(-1, d), *args, TOKEN_TILE, False)
            yp = yp.reshape(bp, s_len, d)
            ys_pad = jnp.pad(ys, ((0, 0), (0, GMLP_CHUNK - t_new), (0, 0))).reshape(-1, d)
            ys_pad, vn = _gmlp_mixer(ys_pad, *args, TOKEN_TILE, True)
            ys = ys_pad.reshape(bs, GMLP_CHUNK, d)[:, :t_new]
            gv_s.append(vn.reshape(bs, GMLP_CHUNK, GMLP_DIM)[:, :t_new])

        yp, mk, mv = _mem_attn(yp, (mem_prompt, mem_wk_b, mem_wv_b), layer, mem_wq_b, mem_wo_b,
                               alpha, 2 * TOKEN_TILE, even_p)
        mk_p.append(mk)
        mv_p.append(mv)
        ys = _mem_attn(ys, (cache_mem_k, cache_mem_v), layer, mem_wq_b, mem_wo_b, alpha, t_new, even_s)

        ffn_args = (row2(lg[1]), row2(lb[1]), layer, ffn_wg_b, ffn_wu_b, ffn_wd_b, row2(lg[2]), row2(lb[2]),
                    alpha)
        yp = _ffn(yp.reshape(-1, d), *ffn_args, FFN_ROWS_PER_STEP).reshape(bp, s_len, d)
        ys = _ffn(ys.reshape(-1, d), *ffn_args, bs * t_new).reshape(bs, t_new, d)

    return (yp, ys, jnp.stack(fk_p), jnp.stack(fv_p), jnp.stack(fl_p), jnp.stack(cs_p),
            jnp.stack(mk_p), jnp.stack(mv_p),
            jnp.stack(fk_s), jnp.stack(fv_s), jnp.stack(fl_s), jnp.stack(cs_s), jnp.stack(gv_s))
```

```python
import functools

import jax
import jax.numpy as jnp
from jax import lax
from jax.experimental import pallas as pl
from jax.experimental.pallas import tpu as pltpu

F32 = jnp.float32
BF16 = jnp.bfloat16

D_MODEL = 1024
FOX_HEADS = 8
FOX_HEAD_DIM = 64
FOX_DIM = FOX_HEADS * FOX_HEAD_DIM
CONV_DIM = 512
CONV_WIDTH = 3
GMLP_CHUNK = 128
GMLP_GROUPS = 8
GMLP_DIM = D_MODEL
MEM_HEADS = 4
MEM_HEAD_DIM = 128
MEM_DIM = MEM_HEADS * MEM_HEAD_DIM
LN_EPS = 1e-5

LANES = 128
SUBLANES = 8
VMEM_LIMIT_BYTES = 56 * 1024 * 1024
TOKEN_TILE = 512
FOX_TILE = 512
FOX_KEY_BLOCKS = 2
FOX_QUERY_BLOCKS = 1
FFN_CHUNK = 256
FFN_ROWS_PER_STEP = 2 * TOKEN_TILE
FFN_BLOCK_ROWS = 256
LOG2E = 1.4426950408889634
CAST_BLOCK_BYTES = 1024 * 1024


def _params(*semantics):
    return pltpu.CompilerParams(dimension_semantics=semantics,
                                vmem_limit_bytes=VMEM_LIMIT_BYTES)


def _resident(shape, layer=None):
    if layer is None:
        nd = len(shape)
        return pl.BlockSpec(shape, lambda *_: (0,) * nd, pipeline_mode=pl.Buffered(1))
    nd = len(shape) - 1
    return pl.BlockSpec((None,) + tuple(shape[1:]), lambda *_: (layer,) + (0,) * nd,
                        pipeline_mode=pl.Buffered(1))


def _dot(a, b):
    return jnp.dot(a, b, preferred_element_type=F32)


def _dot_nt(a, b):
    return lax.dot_general(a, b, (((1,), (1,)), ((), ())), preferred_element_type=F32)


def _layer_norm(y, g, b):
    mu = jnp.mean(y, axis=-1, keepdims=True)
    d = y - mu
    var = jnp.mean(d * d, axis=-1, keepdims=True)
    return d * lax.rsqrt(var + LN_EPS) * g + b


def _gelu(z):
    return 0.5 * z * (1.0 + lax.erf(z * (2.0 ** -0.5)))


def _cast_kernel(*refs, n_blocks):
    n = len(n_blocks)
    i = pl.program_id(0)
    for k in range(n):
        @pl.when(i < n_blocks[k])
        def _(k=k):
            refs[n + k][...] = refs[k][...].astype(BF16)


def _to_bf16(ws):
    flat, specs, n_blocks = [], [], []
    for w in ws:
        n = w.shape[-1]
        rows = w.size // n
        blk = rows
        while blk * n * w.dtype.itemsize > CAST_BLOCK_BYTES and blk % (4 * SUBLANES) == 0:
            blk //= 2
        nb = rows // blk
        flat.append(w.reshape(rows, n))
        specs.append(pl.BlockSpec((blk, n), lambda i, nb=nb: (jnp.minimum(i, nb - 1), 0)))
        n_blocks.append(nb)
    outs = pl.pallas_call(
        functools.partial(_cast_kernel, n_blocks=tuple(n_blocks)),
        grid=(max(n_blocks),), in_specs=specs, out_specs=tuple(specs),
        out_shape=tuple(jax.ShapeDtypeStruct(f.shape, BF16) for f in flat),
        compiler_params=_params("arbitrary"), name="to_bf16",
    )(*flat)
    return [o.reshape(w.shape) for o, w in zip(outs, ws)]


def _lane_prefix_sum(x, carry):
    rows, n = x.shape
    lane = lax.broadcasted_iota(jnp.int32, (rows, LANES), 1)
    segs = []
    for j in range(n // LANES):
        seg = x[:, j * LANES:(j + 1) * LANES]
        step = 1
        while step < LANES:
            seg = seg + jnp.where(lane >= step, pltpu.roll(seg, step, 1), 0.0)
            step *= 2
        segs.append(seg)
    outs = []
    for seg in segs:
        seg = seg + carry
        outs.append(seg)
        carry = seg[:, LANES - 1:LANES]
    return jnp.concatenate(outs, axis=1), carry


def _even_inproj_kernel(x_ref, wqkv_ref, wft_ref, wc_ref, bf_ref, cw_ref, st_ref,
                        q_ref, kf_ref, vf_ref, kb_ref, vb_ref, lf_ref, gc_ref, cs_ref, *rest,
                        tm, nt, emit_c):
    if emit_c:
        c_ref, prev_ref, ccarry_ref = rest
    else:
        (prev_ref,) = rest
    t = pl.program_id(1)

    @pl.when(t == 0)
    def _():
        prev_ref[...] = st_ref[0]
        if emit_c:
            ccarry_ref[...] = jnp.zeros_like(ccarry_ref)

    n_sub = max(1, tm // TOKEN_TILE)
    tb = tm // n_sub
    prev = prev_ref[...]
    carry = ccarry_ref[...] if emit_c else None
    cw = cw_ref[...]
    row = lax.broadcasted_iota(jnp.int32, (tb, CONV_DIM), 0)
    for sb in range(n_sub):
        rows = slice(sb * tb, (sb + 1) * tb)
        xb = x_ref[0, rows, :].astype(BF16)
        lf = jax.nn.log_sigmoid(_dot_nt(wft_ref[...], xb) + bf_ref[...])
        lf_ref[0, :, rows] = lf
        if emit_c:
            c, carry = _lane_prefix_sum(lf, carry)
            c_ref[0, :, rows] = c

        h = _dot(xb, wc_ref[:, 0:CONV_DIM])
        bg = _dot(xb, wc_ref[:, CONV_DIM:2 * CONV_DIM])
        cg = _dot(xb, wc_ref[:, 2 * CONV_DIM:3 * CONV_DIM])
        pre = cg * h

        p1 = jnp.where(row == 0, prev[7:8, :], pltpu.roll(pre, 1, 0))
        p2 = jnp.where(row == 0, prev[6:7, :],
                       jnp.where(row == 1, prev[7:8, :], pltpu.roll(pre, 2, 0)))
        conv = cw[0:1, :] * p2 + cw[1:2, :] * p1 + cw[2:3, :] * pre
        gc_ref[0, rows, :] = (bg * conv).astype(BF16)
        prev = pre[tb - SUBLANES:tb, :]

        q = _dot(xb, wqkv_ref[:, 0:FOX_DIM])
        q_ref[0, rows, :] = (q * (LOG2E * FOX_HEAD_DIM ** -0.5)).astype(BF16)
        k = _dot(xb, wqkv_ref[:, FOX_DIM:2 * FOX_DIM])
        kf_ref[0, rows, :] = k
        kb_ref[0, rows, :] = k.astype(BF16)
        v = _dot(xb, wqkv_ref[:, 2 * FOX_DIM:3 * FOX_DIM])
        vf_ref[0, rows, :] = v
        vb_ref[0, rows, :] = v.astype(BF16)
    prev_ref[...] = prev
    if emit_c:
        ccarry_ref[...] = carry

    @pl.when(t == nt - 1)
    def _():
        cs_ref[0] = prev[SUBLANES - (CONV_WIDTH - 1):SUBLANES, :]


def _even_inproj(x, wqkv, wft, wc, bf, cw, state8, tm, emit_c):
    bn, tlen, _ = x.shape
    nt = tlen // tm
    tok = lambda w: pl.BlockSpec((1, tm, w), lambda b, t: (b, t, 0))
    tlane = pl.BlockSpec((1, FOX_HEADS, tm), lambda b, t: (b, 0, t))
    out_shape = [
        jax.ShapeDtypeStruct((bn, tlen, FOX_DIM), BF16),
        jax.ShapeDtypeStruct((bn, tlen, FOX_DIM), F32),
        jax.ShapeDtypeStruct((bn, tlen, FOX_DIM), F32),
        jax.ShapeDtypeStruct((bn, tlen, FOX_DIM), BF16),
        jax.ShapeDtypeStruct((bn, tlen, FOX_DIM), BF16),
        jax.ShapeDtypeStruct((bn, FOX_HEADS, tlen), F32),
        jax.ShapeDtypeStruct((bn, tlen, CONV_DIM), BF16),
        jax.ShapeDtypeStruct((bn, CONV_WIDTH - 1, CONV_DIM), F32),
    ]
    out_specs = [tok(FOX_DIM), tok(FOX_DIM), tok(FOX_DIM), tok(FOX_DIM), tok(FOX_DIM), tlane,
                 tok(CONV_DIM), pl.BlockSpec((1, CONV_WIDTH - 1, CONV_DIM), lambda b, t: (b, 0, 0))]
    scratch = [pltpu.VMEM((SUBLANES, CONV_DIM), F32)]
    if emit_c:
        out_shape.append(jax.ShapeDtypeStruct((bn, FOX_HEADS, tlen), F32))
        out_specs.append(tlane)
        scratch.append(pltpu.VMEM((FOX_HEADS, 1), F32))
    return pl.pallas_call(
        functools.partial(_even_inproj_kernel, tm=tm, nt=nt, emit_c=emit_c),
        grid=(bn, nt),
        in_specs=[tok(D_MODEL), _resident(wqkv.shape), _resident(wft.shape),
                  _resident(wc.shape), _resident(bf.shape), _resident(cw.shape),
                  pl.BlockSpec((1, SUBLANES, CONV_DIM), lambda b, t: (b, 0, 0))],
        out_specs=tuple(out_specs),
        out_shape=tuple(out_shape),
        scratch_shapes=scratch,
        compiler_params=_params("parallel", "arbitrary"),
        name="even_inproj",
    )(x, wqkv, wft, wc, bf, cw, state8)


def _head_mask(shape, hh):
    lane = lax.broadcasted_iota(jnp.int32, shape, 1)
    return (lane >= FOX_HEAD_DIM) if hh == 1 else (lane < FOX_HEAD_DIM)


def _fox_prompt_kernel(q_ref, k_ref, v_ref, c_ref, o_ref, vt_ref, cb_ref, *, tq, nq):
    hp = pl.program_id(1)
    i = pl.program_id(2)
    half = lax.broadcasted_iota(jnp.int32, (LANES, tq), 0) < FOX_HEAD_DIM

    @pl.when(i == 0)
    def _():
        for j in range(nq):
            rows = slice(j * tq, (j + 1) * tq)
            vt = v_ref[0, rows, :].astype(F32).T
            for hh in range(2):
                own = half if hh == 0 else jnp.logical_not(half)
                vt_ref[hh, j] = jnp.where(own, vt, 1.0).astype(BF16)
                c_j = c_ref[0, (hp * 2 + hh) * nq + j]
                rel = (c_j[:, 0:1] - c_j) * LOG2E
                cb_ref[hh, rows, :] = jnp.broadcast_to(rel, (LANES, tq)).T

    q = q_ref[0]
    qh = [jnp.where(_head_mask(q.shape, hh), q, jnp.zeros_like(q)) for hh in range(2)]
    wk = tq // FOX_KEY_BLOCKS
    wq = tq // FOX_QUERY_BLOCKS
    key = lax.broadcasted_iota(jnp.int32, (wk, wq), 0)
    qry = lax.broadcasted_iota(jnp.int32, (wk, wq), 1)
    streams = [(hh, qb) for hh in range(2) for qb in range(FOX_QUERY_BLOCKS)]

    def c_start(j, hh):
        return c_ref[0, (hp * 2 + hh) * nq + j][:, 0:1]

    def scores(j, kb, hh, qb):
        rows = pl.ds(pl.multiple_of(j * tq + kb * wk, wk), wk)
        st = _dot_nt(k_ref[0, rows, :], qh[hh][qb * wq:(qb + 1) * wq, :])
        return st + jnp.tile(cb_ref[hh, rows, :], (1, wq // LANES))

    def chain(tiles, carry, diagonal=False):
        blocks = [(i, kb, True) for kb in range(FOX_KEY_BLOCKS)] if diagonal else []
        blocks += [(j, kb, False) for j in tiles for kb in range(FOX_KEY_BLOCKS)]
        live = lambda kb, qb, diag: not diag or kb * wk < (qb + 1) * wq
        sts = [[scores(j, kb, hh, qb) if live(kb, qb, diag) else None for hh, qb in streams]
               for j, kb, diag in blocks]
        carry = [None] * len(streams) if carry is None else list(carry)
        for (j, kb, diag), st_b in zip(blocks, sts):
            for n, (hh, qb) in enumerate(streams):
                st = st_b[n]
                if st is None:
                    continue
                if diag:
                    if (kb + 1) * wk - 1 > qb * wq:
                        st = jnp.where(key + kb * wk <= qry + qb * wq, st, -jnp.inf)
                    top = jnp.max(st, axis=0, keepdims=True)
                else:
                    delta = (c_start(i, hh) - c_start(j, hh)) * LOG2E
                    top = jnp.max(st, axis=0, keepdims=True) + delta
                m_new = top if carry[n] is None else jnp.maximum(carry[n][0], top)
                p = jnp.exp2(st - (m_new if diag else m_new - delta)).astype(BF16)
                pv = _dot(vt_ref[hh, j, :, kb * wk:(kb + 1) * wk], p)
                if carry[n] is None:
                    carry[n] = (m_new, pv)
                else:
                    carry[n] = (m_new, jnp.exp2(carry[n][0] - m_new) * carry[n][1] + pv)
        return tuple(carry)

    carry = lax.cond(i % 2 == 1, lambda: chain([i - 1], None, True), lambda: chain([], None, True))
    carry = lax.fori_loop(0, i // 2, lambda jj, c: chain([2 * jj, 2 * jj + 1], c), carry)
    accs = [jnp.concatenate([carry[hh * FOX_QUERY_BLOCKS + qb][1] for qb in range(FOX_QUERY_BLOCKS)], axis=1)
            for hh in range(2)]
    out0 = accs[0] / accs[0][FOX_HEAD_DIM:FOX_HEAD_DIM + 1, :]
    out1 = accs[1] / accs[1][0:1, :]
    o_ref[0] = jnp.where(half, out0, out1).T.astype(o_ref.dtype)


def _fox_prompt(q, k, v, c, tq):
    bn, s_len, _ = q.shape
    nq = s_len // tq
    c4 = c.reshape(bn, FOX_HEADS * nq, 1, tq)
    qspec = pl.BlockSpec((1, tq, LANES), lambda b, h, i: (b, i, h))
    kspec = pl.BlockSpec((1, s_len, LANES), lambda b, h, i: (b, 0, h))
    return pl.pallas_call(
        functools.partial(_fox_prompt_kernel, tq=tq, nq=nq),
        grid=(bn, FOX_DIM // LANES, nq),
        in_specs=[qspec, kspec, kspec,
                  pl.BlockSpec((1, FOX_HEADS * nq, 1, tq), lambda b, h, i: (b, 0, 0, 0))],
        out_specs=qspec,
        out_shape=jax.ShapeDtypeStruct((bn, s_len, FOX_DIM), BF16),
        scratch_shapes=[pltpu.VMEM((2, nq, LANES, tq), BF16),
                        pltpu.VMEM((2, s_len, LANES), F32)],
        compiler_params=_params("parallel", "parallel", "arbitrary"),
        name="fox_prompt",
    )(q, k, v, c4)


def _fox_sample_kernel(q_ref, kn_ref, vn_ref, kc_ref, vc_ref, lf_ref, o_ref, *, t_new, p_len):
    n_new = lf_ref.shape[-1] - p_len
    zeros = jnp.zeros((n_new - t_new, LANES), BF16)
    row = lax.broadcasted_iota(jnp.int32, (t_new, n_new), 0)
    col = lax.broadcasted_iota(jnp.int32, (t_new, n_new), 1)
    c_all, _ = _lane_prefix_sum(lf_ref[0], jnp.zeros((FOX_HEADS, 1), F32))
    for hp in range(FOX_DIM // LANES):
        lanes = slice(hp * LANES, (hp + 1) * LANES)
        q = q_ref[0, :, lanes]
        kn = jnp.concatenate([kn_ref[0, :, lanes], zeros], axis=0)
        vn = jnp.concatenate([vn_ref[0, :, lanes], zeros], axis=0)
        kc = kc_ref[0, :, lanes].astype(BF16)
        vc = vc_ref[0, :, lanes].astype(BF16)
        outs = []
        for hh in range(2):
            head = hp * 2 + hh
            qh = jnp.where(_head_mask(q.shape, hh), q, jnp.zeros_like(q))
            c_row = c_all[head:head + 1, :]
            bias = (c_row[:, p_len - 1:p_len] - c_row) * LOG2E
            s_c = _dot_nt(qh, kc) + bias[:, 0:p_len]
            s_n = jnp.where(col <= row, _dot_nt(qh, kn) + bias[:, p_len:], -jnp.inf)
            m = jnp.maximum(jnp.max(s_c, axis=-1, keepdims=True), jnp.max(s_n, axis=-1, keepdims=True))
            p_c = jnp.exp2(s_c - m)
            p_n = jnp.exp2(s_n - m)
            l = jnp.sum(p_c, axis=-1, keepdims=True) + jnp.sum(p_n, axis=-1, keepdims=True)
            outs.append((_dot(p_c.astype(BF16), vc) + _dot(p_n.astype(BF16), vn)) / l)
        o_ref[0, :, lanes] = jnp.where(_head_mask(outs[0].shape, 0), outs[0], outs[1]).astype(o_ref.dtype)


def _fox_sample(q, kn, vn, kc, vc, lf_all):
    bn, t_new, _ = q.shape
    p_len = kc.shape[1]
    n_keys = lf_all.shape[-1]
    qspec = pl.BlockSpec((1, t_new, FOX_DIM), lambda b: (b, 0, 0))
    cspec = pl.BlockSpec((1, p_len, FOX_DIM), lambda b: (b, 0, 0))
    return pl.pallas_call(
        functools.partial(_fox_sample_kernel, t_new=t_new, p_len=p_len),
        grid=(bn,),
        in_specs=[qspec, qspec, qspec, cspec, cspec,
                  pl.BlockSpec((1, FOX_HEADS, n_keys), lambda b: (b, 0, 0))],
        out_specs=qspec,
        out_shape=jax.ShapeDtypeStruct((bn, t_new, FOX_DIM), BF16),
        compiler_params=_params("parallel"),
        name="fox_sample",
    )(q, kn, vn, kc, vc, lf_all)


def _gmlp_kernel(x_ref, win_ref, bin_ref, ng_ref, nb_ref, ws_ref, bst_ref, wout_ref,
                 g_ref, b_ref, y_ref, *rest, tm, alpha, emit_vn, n_sub):
    if emit_vn:
        vn_ref, u_scr, vnb_scr, gated_scr = rest
    else:
        u_scr, vnb_scr, gated_scr = rest
    row = lax.broadcasted_iota(jnp.int32, (GMLP_CHUNK, GMLP_CHUNK), 0)
    col = lax.broadcasted_iota(jnp.int32, (GMLP_CHUNK, GMLP_CHUNK), 1)
    gd = GMLP_DIM // GMLP_GROUPS
    rows_per = tm // n_sub
    for sb in range(n_sub):
        rows = slice(sb * rows_per, (sb + 1) * rows_per)
        x = x_ref[rows, :]
        xb = x.astype(BF16)
        v = _gelu(_dot(xb, win_ref[:, GMLP_DIM:]) + bin_ref[:, GMLP_DIM:])
        vn = _layer_norm(v, ng_ref[...], nb_ref[...])
        if emit_vn:
            vn_ref[rows, :] = vn
        vnb_scr[rows, :] = vn.astype(BF16)
        u_scr[rows, :] = _gelu(_dot(xb, win_ref[:, 0:GMLP_DIM]) + bin_ref[:, 0:GMLP_DIM])

        chunks = range(sb * rows_per // GMLP_CHUNK, (sb + 1) * rows_per // GMLP_CHUNK)
        for g in range(GMLP_GROUPS):
            wm = jnp.where(col <= row, ws_ref[g], 0.0).astype(BF16)
            bias = bst_ref[:, g:g + 1]
            cs = slice(g * gd, (g + 1) * gd)
            for r in range(chunks.start, chunks.stop, 2):
                rss = [slice(rr * GMLP_CHUNK, (rr + 1) * GMLP_CHUNK) for rr in range(r, min(r + 2, chunks.stop))]
                mixed = _dot(wm, jnp.concatenate([vnb_scr[rs, cs] for rs in rss], axis=1)) + bias
                for n, rs in enumerate(rss):
                    gated_scr[rs, cs] = (u_scr[rs, cs] * mixed[:, n * gd:(n + 1) * gd]).astype(BF16)
        out = _dot(gated_scr[rows, :], wout_ref[...])
        y_ref[rows, :] = _layer_norm(alpha * x + out, g_ref[...], b_ref[...])


def _gmlp_mixer(x, layer, win, bin_, ng, nb, ws, bst, wout, g, b, alpha, tm, emit_vn):
    n = x.shape[0]
    tok = pl.BlockSpec((tm, D_MODEL), lambda i: (i, 0))
    out_shape = [jax.ShapeDtypeStruct((n, D_MODEL), F32)]
    out_specs = [tok]
    if emit_vn:
        out_shape.append(jax.ShapeDtypeStruct((n, GMLP_DIM), F32))
        out_specs.append(tok)
    res = pl.pallas_call(
        functools.partial(_gmlp_kernel, tm=tm, alpha=alpha, emit_vn=emit_vn, n_sub=max(1, tm // TOKEN_TILE)),
        grid=(n // tm,),
        in_specs=[tok, _resident(win.shape, layer), _resident(bin_.shape), _resident(ng.shape),
                  _resident(nb.shape), _resident(ws.shape, layer), _resident(bst.shape),
                  _resident(wout.shape, layer), _resident(g.shape), _resident(b.shape)],
        out_specs=tuple(out_specs),
        out_shape=tuple(out_shape),
        scratch_shapes=[pltpu.VMEM((tm, GMLP_DIM), F32), pltpu.VMEM((tm, GMLP_DIM), BF16),
                        pltpu.VMEM((tm, GMLP_DIM), BF16)],
        compiler_params=_params("parallel"),
        name="gmlp_mixer",
    )(x, win, bin_, ng, nb, ws, bst, wout, g, b)
    return res if emit_vn else (res[0], None)


def _mem_attn_kernel(x_ref, *refs, alpha, fuse_even, project_kv, n_sub):
    if fuse_even:
        att_ref, gc_ref, wmix_ref, g0_ref, b0_ref = refs[:5]
        refs = refs[5:]
    if project_kv:
        (mem_ref, wk_ref, wv_ref, wq_ref, wo_ref,
         y_ref, mk_ref, mv_ref, mkb_ref, mvb_ref) = refs
    else:
        mk_ref, mv_ref, wq_ref, wo_ref, y_ref, mkb_ref, mvb_ref = refs

    @pl.when(pl.program_id(1) == 0)
    def _():
        if project_kv:
            mb = mem_ref[0].astype(BF16)
            k = _dot(mb, wk_ref[...])
            v = _dot(mb, wv_ref[...])
        for h in range(MEM_HEADS):
            hs = slice(h * MEM_HEAD_DIM, (h + 1) * MEM_HEAD_DIM)
            if project_kv:
                mk_ref[0, :, h, :] = k[:, hs]
                mv_ref[0, :, h, :] = v[:, hs]
                mkb_ref[h] = k[:, hs].astype(BF16)
                mvb_ref[h] = v[:, hs].astype(BF16)
            else:
                mkb_ref[h] = mk_ref[0, 0, :, h, :].astype(BF16)
                mvb_ref[h] = mv_ref[0, 0, :, h, :].astype(BF16)

    rows_per = x_ref.shape[1] // n_sub
    for sb in range(n_sub):
        rows = slice(sb * rows_per, (sb + 1) * rows_per)
        x = x_ref[0, rows, :]
        if fuse_even:
            mix = _dot(att_ref[0, rows, :], wmix_ref[0:FOX_DIM, :]) + _dot(gc_ref[0, rows, :], wmix_ref[FOX_DIM:, :])
            x = _layer_norm(alpha * x + mix, g0_ref[...], b0_ref[...])
        q = _dot(x.astype(BF16), wq_ref[...]) * (LOG2E * MEM_HEAD_DIM ** -0.5)
        qb = q.astype(BF16)
        heads = []
        for h in range(MEM_HEADS):
            hs = slice(h * MEM_HEAD_DIM, (h + 1) * MEM_HEAD_DIM)
            s = _dot_nt(qb[:, hs], mkb_ref[h])
            m = jnp.max(s, axis=-1, keepdims=True)
            p = jnp.exp2(s - m)
            l = jnp.sum(p, axis=-1, keepdims=True)
            heads.append((_dot(p.astype(BF16), mvb_ref[h]) / l).astype(BF16))
        o = jnp.concatenate(heads, axis=-1)
        y_ref[0, rows, :] = alpha * x + _dot(o, wo_ref[...])


def _mem_attn(x, kv, layer, wq, wo, alpha, tm, even=None):
    bn, tlen, _ = x.shape
    project_kv = len(kv) == 3
    mlen = kv[0].shape[1] if project_kv else kv[0].shape[2]
    tok = lambda w: pl.BlockSpec((1, tm, w), lambda bb, t: (bb, t, 0))
    if project_kv:
        kv_specs = [pl.BlockSpec((1, mlen, D_MODEL), lambda bb, t: (bb, 0, 0)),
                    _resident(kv[1].shape, layer), _resident(kv[2].shape, layer)]
    else:
        kv_specs = [pl.BlockSpec((1, 1, mlen, MEM_HEADS, MEM_HEAD_DIM), lambda bb, t: (layer, bb, 0, 0, 0))] * 2
    pre_args, pre_specs = [], []
    if even is not None:
        att, gc, wmix, e, g0, b0 = even
        pre_args = [att, gc, wmix, g0, b0]
        pre_specs = [tok(FOX_DIM), tok(CONV_DIM), _resident(wmix.shape, e), _resident(g0.shape),
                     _resident(b0.shape)]
    out_specs = [tok(D_MODEL)]
    out_shape = [jax.ShapeDtypeStruct((bn, tlen, D_MODEL), F32)]
    if project_kv:
        out_specs += [pl.BlockSpec((1, mlen, MEM_HEADS, MEM_HEAD_DIM), lambda bb, t: (bb, 0, 0, 0))] * 2
        out_shape += [jax.ShapeDtypeStruct((bn, mlen, MEM_HEADS, MEM_HEAD_DIM), F32)] * 2
    res = pl.pallas_call(
        functools.partial(_mem_attn_kernel, alpha=alpha, fuse_even=even is not None,
                          project_kv=project_kv, n_sub=max(1, tm // TOKEN_TILE)),
        grid=(bn, tlen // tm),
        in_specs=[tok(D_MODEL)] + pre_specs + kv_specs + [_resident(wq.shape, layer), _resident(wo.shape, layer)],
        out_specs=tuple(out_specs),
        out_shape=tuple(out_shape),
        scratch_shapes=[pltpu.VMEM((MEM_HEADS, mlen, MEM_HEAD_DIM), BF16),
                        pltpu.VMEM((MEM_HEADS, mlen, MEM_HEAD_DIM), BF16)],
        compiler_params=_params("parallel", "arbitrary"),
        name="mem_attn",
    )(x, *pre_args, *kv, wq, wo)
    return res if project_kv else res[0]


def _ffn_kernel(x_ref, gin_ref, bin_ref, wg_ref, wu_ref, wd_ref, g_ref, b_ref, y_ref, h_scr, *, alpha, n_sub):
    d_ff = wg_ref.shape[1]
    rows_per = x_ref.shape[0] // n_sub
    for s in range(n_sub):
        rows = slice(s * rows_per, (s + 1) * rows_per)
        x = _layer_norm(x_ref[rows, :], gin_ref[...], bin_ref[...])
        xb = x.astype(BF16)
        for c in range(d_ff // FFN_CHUNK):
            cols = slice(c * FFN_CHUNK, (c + 1) * FFN_CHUNK)
            gate = _dot(xb, wg_ref[:, cols])
            up = _dot(xb, wu_ref[:, cols])
            h_scr[rows, cols] = (gate * jax.nn.sigmoid(gate) * up).astype(BF16)
        y = alpha * x + _dot(h_scr[rows, :], wd_ref[...])
        y_ref[rows, :] = _layer_norm(y, g_ref[...], b_ref[...])


def _ffn(x, g_in, b_in, layer, wg, wu, wd, g, b, alpha, tm):
    n = x.shape[0]
    d_ff = wg.shape[-1]
    assert d_ff % FFN_CHUNK == 0
    tok = pl.BlockSpec((tm, D_MODEL), lambda i: (i, 0))
    return pl.pallas_call(
        functools.partial(_ffn_kernel, alpha=alpha, n_sub=max(1, tm // FFN_BLOCK_ROWS)),
        grid=(n // tm,),
        in_specs=[tok, _resident(g_in.shape), _resident(b_in.shape), _resident(wg.shape, layer),
                  _resident(wu.shape, layer), _resident(wd.shape, layer), _resident(g.shape), _resident(b.shape)],
        out_specs=tok,
        out_shape=jax.ShapeDtypeStruct((n, D_MODEL), F32),
        scratch_shapes=[pltpu.VMEM((tm, d_ff), BF16)],
        compiler_params=_params("parallel"),
        name="swiglu_ffn",
    )(x, g_in, b_in, wg, wu, wd, g, b)


def _ffn_stream_kernel(x_ref, gin_ref, bin_ref, wg_ref, wu_ref, wd_ref, g_ref, b_ref, y_ref,
                       xn_scr, xb_scr, acc_scr, *, alpha):
    c = pl.program_id(0)

    @pl.when(c == 0)
    def _():
        xn = _layer_norm(x_ref[...], gin_ref[...], bin_ref[...])
        xn_scr[...] = xn
        xb_scr[...] = xn.astype(BF16)
        acc_scr[...] = jnp.zeros_like(acc_scr)

    xb = xb_scr[...]
    gate = _dot(xb, wg_ref[...])
    up = _dot(xb, wu_ref[...])
    h = (gate * jax.nn.sigmoid(gate) * up).astype(BF16)
    acc_scr[...] += _dot(h, wd_ref[...])

    @pl.when(c == pl.num_programs(0) - 1)
    def _():
        y_ref[...] = _layer_norm(alpha * xn_scr[...] + acc_scr[...], g_ref[...], b_ref[...])


def _ffn_stream(x, g_in, b_in, layer, wg, wu, wd, g, b, alpha):
    n = x.shape[0]
    d_ff = wg.shape[-1]
    assert d_ff % FFN_CHUNK == 0
    tok = pl.BlockSpec((n, D_MODEL), lambda c: (0, 0))
    col = pl.BlockSpec((None, D_MODEL, FFN_CHUNK), lambda c: (layer, 0, c))
    return pl.pallas_call(
        functools.partial(_ffn_stream_kernel, alpha=alpha),
        grid=(d_ff // FFN_CHUNK,),
        in_specs=[tok, _resident(g_in.shape), _resident(b_in.shape), col, col,
                  pl.BlockSpec((None, FFN_CHUNK, D_MODEL), lambda c: (layer, c, 0)),
                  _resident(g.shape), _resident(b.shape)],
        out_specs=tok,
        out_shape=jax.ShapeDtypeStruct((n, D_MODEL), F32),
        scratch_shapes=[pltpu.VMEM((n, D_MODEL), F32), pltpu.VMEM((n, D_MODEL), BF16),
                        pltpu.VMEM((n, D_MODEL), F32)],
        compiler_params=_params("arbitrary"),
        name="swiglu_ffn_stream",
    )(x, g_in, b_in, wg, wu, wd, g, b)


def kernel(x_prompt, x_sample, cache_fox_k, cache_fox_v, cache_fox_logf, state_conv, cache_mem_k, cache_mem_v, mem_prompt, w_in_even, b_forget, conv_w, w_out_even, w_in_odd, b_in_odd, gmlp_norm_g, gmlp_norm_b, gmlp_w_s, gmlp_b_s, w_out_odd, mem_w_q, mem_w_k, mem_w_v, mem_w_o, ffn_w_gate, ffn_w_up, ffn_w_down, ln_g, ln_b):
    depth = ln_g.shape[0]
    alpha = float((2 * depth) ** 0.25)
    bp, s_len, d = x_prompt.shape
    bs, t_new, _ = x_sample.shape
    p_len = cache_fox_k.shape[2]
    row2 = lambda a: a.reshape(1, -1)

    (w_out_even_b, w_in_odd_b, w_out_odd_b, mem_wq_b, mem_wo_b, mem_wk_b, mem_wv_b,
     ffn_wg_b, ffn_wu_b, ffn_wd_b) = _to_bf16([w_out_even, w_in_odd, w_out_odd, mem_w_q, mem_w_o, mem_w_k,
                                               mem_w_v, ffn_w_gate, ffn_w_up, ffn_w_down])

    yp, ys = x_prompt, x_sample
    fk_p, fv_p, fl_p, cs_p, mk_p, mv_p = [], [], [], [], [], []
    fk_s, fv_s, fl_s, cs_s, gv_s = [], [], [], [], []
    for layer in range(depth):
        lg, lb = ln_g[layer], ln_b[layer]
        even_p = even_s = None
        if layer % 2 == 0:
            e = layer // 2
            w_in = w_in_even[e].astype(BF16)
            qkv_w = w_in[:, 0:3 * FOX_DIM]
            f_wt = w_in[:, 3 * FOX_DIM:3 * FOX_DIM + FOX_HEADS].T
            c_w = w_in[:, 3 * FOX_DIM + FOX_HEADS:]
            f_b = b_forget[e].reshape(FOX_HEADS, 1)
            cw = conv_w[e]

            st0 = jnp.zeros((bp, SUBLANES, CONV_DIM), F32)
            q, kf, vf, kb, vb, lf, gc, cst, c = _even_inproj(yp, qkv_w, f_wt, c_w, f_b, cw, st0,
                                                             2 * TOKEN_TILE, True)
            att = _fox_prompt(q, kb, vb, c, FOX_TILE)
            even_p = (att, gc, w_out_even_b, e, row2(lg[0]), row2(lb[0]))
            fk_p.append(kf.reshape(bp, s_len, FOX_HEADS, FOX_HEAD_DIM))
            fv_p.append(vf.reshape(bp, s_len, FOX_HEADS, FOX_HEAD_DIM))
            fl_p.append(jnp.swapaxes(lf, 1, 2))
            cs_p.append(cst)

            st = jnp.pad(state_conv[e], ((0, 0), (SUBLANES - (CONV_WIDTH - 1), 0), (0, 0)))
            q, kf, vf, kb, vb, lf, gc, cst = _even_inproj(ys, qkv_w, f_wt, c_w, f_b, cw, st, t_new, False)
            pad = -(p_len + t_new) % LANES
            lf_all = jnp.concatenate([jnp.swapaxes(cache_fox_logf[e], 1, 2), lf,
                                      jnp.zeros((bs, FOX_HEADS, pad), F32)], axis=2)
            att = _fox_sample(q, kb, vb, cache_fox_k[e].reshape(bs, p_len, FOX_DIM),
                              cache_fox_v[e].reshape(bs, p_len, FOX_DIM), lf_all)
            even_s = (att, gc, w_out_even_b, e, row2(lg[0]), row2(lb[0]))
            fk_s.append(kf.reshape(bs, t_new, FOX_HEADS, FOX_HEAD_DIM))
            fv_s.append(vf.reshape(bs, t_new, FOX_HEADS, FOX_HEAD_DIM))
            fl_s.append(jnp.swapaxes(lf, 1, 2))
            cs_s.append(cst)
        else:
            o = layer // 2
            args = (o, w_in_odd_b, row2(b_in_odd[o]), row2(gmlp_norm_g[o]), row2(gmlp_norm_b[o]),
                    gmlp_w_s, gmlp_b_s[o].T, w_out_odd_b, row2(lg[0]), row2(lb[0]), alpha)
            yp, _ = _gmlp_mixer(yp.reshape(-1, d), *args, TOKEN_TILE, False)
            yp = yp.reshape(bp, s_len, d)
            ys_pad = jnp.pad(ys, ((0, 0), (0, GMLP_CHUNK - t_new), (0, 0))).reshape(-1, d)
            ys_pad, vn = _gmlp_mixer(ys_pad, *args, TOKEN_TILE, True)
            ys = ys_pad.reshape(bs, GMLP_CHUNK, d)[:, :t_new]
            gv_s.append(vn.reshape(bs, GMLP_CHUNK, GMLP_DIM)[:, :t_new])

        yp, mk, mv = _mem_attn(yp, (mem_prompt, mem_wk_b, mem_wv_b), layer, mem_wq_b, mem_wo_b,
                               alpha, 2 * TOKEN_TILE, even_p)
        mk_p.append(mk)
        mv_p.append(mv)
        ys = _mem_attn(ys, (cache_mem_k, cache_mem_v), layer, mem_wq_b, mem_wo_b, alpha, t_new, even_s)

        ffn_args = (row2(lg[1]), row2(lb[1]), layer, ffn_wg_b, ffn_wu_b, ffn_wd_b, row2(lg[2]), row2(lb[2]),
                    alpha)
        yp = _ffn(yp.reshape(-1, d), *ffn_args, FFN_ROWS_PER_STEP).reshape(bp, s_len, d)
        ys = _ffn_stream(ys.reshape(-1, d), *ffn_args).reshape(bs, t_new, d)

    return (yp, ys, jnp.stack(fk_p), jnp.stack(fv_p), jnp.stack(fl_p), jnp.stack(cs_p),
            jnp.stack(mk_p), jnp.stack(mv_p),
            jnp.stack(fk_s), jnp.stack(fv_s), jnp.stack(fl_s), jnp.stack(cs_s), jnp.stack(gv_s))
```
